```python
import math
import jax
import jax.numpy as jnp
from jax import lax
import numpy as np

D_MODEL = 1024
BATCH = 4
SEQ = 8192
DEPTH = 1

GRID_W = 64
CTX_LEN = 256

ATTN_HEADS = 8
ATTN_KV_HEADS = 2
ATTN_GROUP = ATTN_HEADS // ATTN_KV_HEADS
ATTN_HEAD_DIM = 64
WINDOW = 128
ATTN_BLOCK = 128
ROPE_BASE = 10000.0
DN_HEADS = 4
DN_HEAD_DIM = 128
DN_CONV = 5
DN_CHUNK = 64
ATTN_Q_W = ATTN_HEADS * ATTN_HEAD_DIM
ATTN_KV_W = ATTN_KV_HEADS * ATTN_HEAD_DIM
DN_W = DN_HEADS * DN_HEAD_DIM
MIX_W = ATTN_Q_W + DN_W
D_IN = ATTN_Q_W + 2 * ATTN_KV_W + 4 * DN_W + 4 * DN_HEADS
IN_SPLITS = (ATTN_Q_W, ATTN_Q_W + ATTN_KV_W, ATTN_Q_W + 2 * ATTN_KV_W,
             ATTN_Q_W + 2 * ATTN_KV_W + 3 * DN_W, ATTN_Q_W + 2 * ATTN_KV_W + 4 * DN_W)
PEER_HEADS = 8
PEER_N_KEYS = 128
PEER_EXPERTS = PEER_N_KEYS ** 2
PEER_KEY_HALF = 128
PEER_TOPK = 16
PEER_BLOCK = 128
DEEPNORM_ALPHA = (2 * DEPTH) ** 0.25
DEEPNORM_BETA = (8 * DEPTH) ** -0.25
LN_EPS = 1e-6
NEG_INF = -1e30

kernel_name = "hybrid_swa_gdn_peer_dit_layer"


def standardize(x):
    xf = x.astype(jnp.float32)
    xc = xf - xf.mean(-1, keepdims=True)
    var = (xc * xc).mean(-1, keepdims=True)
    return (xc * lax.rsqrt(var + LN_EPS)).astype(x.dtype)


def layer_norm(x, g, b):
    return standardize(x) * g + b


def modulate(x, shift, scale):
    return standardize(x) * (1 + scale) + shift


def axial_rope(x, row, col):
    half = x.shape[-1] // 2
    nf = half // 2
    freqs = ROPE_BASE ** (-jnp.arange(nf, dtype=jnp.float32) / nf)

    def rotate(xa, pos):
        ang = pos.astype(jnp.float32)[:, None] * freqs
        cos = jnp.cos(ang)[None, :, None, :].astype(x.dtype)
        sin = jnp.sin(ang)[None, :, None, :].astype(x.dtype)
        x1, x2 = xa[..., :nf], xa[..., nf:]
        return jnp.concatenate([x1 * cos - x2 * sin, x1 * sin + x2 * cos], -1)

    return jnp.concatenate([rotate(x[..., :half], row), rotate(x[..., half:], col)], -1)


def softmax_with_sink(logits, sink):
    snk = jnp.broadcast_to(sink.astype(jnp.float32).reshape(ATTN_KV_HEADS, ATTN_GROUP, 1, 1),
                           logits.shape[:-1] + (1,))
    return jax.nn.softmax(jnp.concatenate([logits, snk], -1), axis=-1)[..., :-1]


def banded_window_attention(q, k, v, k_ctx, v_ctx, sink):
    b, n = q.shape[:2]
    nblk = n // ATTN_BLOCK
    scale = ATTN_HEAD_DIM ** -0.5
    qb = q.reshape(b, nblk, ATTN_BLOCK, ATTN_KV_HEADS, ATTN_GROUP, ATTN_HEAD_DIM)
    pad = ((0, 0), (ATTN_BLOCK, ATTN_BLOCK), (0, 0), (0, 0))
    k_pad = jnp.pad(k, pad)
    v_pad = jnp.pad(v, pad)
    n_loc = 3 * ATTN_BLOCK
    q_off = jnp.arange(ATTN_BLOCK)
    k_off = jnp.arange(n_loc) - ATTN_BLOCK
    in_band = jnp.abs(q_off[:, None] - k_off[None, :]) <= WINDOW

    def one_block(i):
        qi = lax.dynamic_index_in_dim(qb, i, axis=1, keepdims=False)
        ki = lax.dynamic_slice_in_dim(k_pad, i * ATTN_BLOCK, n_loc, axis=1)
        vi = lax.dynamic_slice_in_dim(v_pad, i * ATTN_BLOCK, n_loc, axis=1)
        k_abs = i * ATTN_BLOCK + k_off
        mask = in_band & ((k_abs >= 0) & (k_abs < n))[None, :]
        s_loc = jnp.einsum('bqhgd,bkhd->bhgqk', qi, ki).astype(jnp.float32) * scale
        s_loc = jnp.where(mask, s_loc, NEG_INF)
        s_ctx = jnp.einsum('bqhgd,bkhd->bhgqk', qi, k_ctx).astype(jnp.float32) * scale
        p = softmax_with_sink(jnp.concatenate([s_loc, s_ctx], -1), sink).astype(v.dtype)
        return (jnp.einsum('bhgqk,bkhd->bqhgd', p[..., :n_loc], vi)
                + jnp.einsum('bhgqk,bkhd->bqhgd', p[..., n_loc:], v_ctx))

    o = lax.map(one_block, jnp.arange(nblk))
    return jnp.moveaxis(o, 0, 1).reshape(b, n, ATTN_Q_W)


def context_attention(q, k, v, sink):
    b, m = q.shape[:2]
    qg = q.reshape(b, m, ATTN_KV_HEADS, ATTN_GROUP, ATTN_HEAD_DIM)
    s = jnp.einsum('bqhgd,bkhd->bhgqk', qg, k).astype(jnp.float32) * ATTN_HEAD_DIM ** -0.5
    p = softmax_with_sink(s, sink).astype(v.dtype)
    return jnp.einsum('bhgqk,bkhd->bqhgd', p, v).reshape(b, m, ATTN_Q_W)


def centred_conv(x, w):
    pad = (w.shape[0] - 1) // 2
    return lax.conv_general_dilated(x, w[:, None, :], window_strides=(1,), padding=[(pad, pad)],
                                    dimension_numbers=('NWC', 'WIO', 'NWC'),
                                    feature_group_count=x.shape[-1])


def l2norm(x):
    xf = x.astype(jnp.float32)
    return xf * lax.rsqrt(jnp.sum(xf * xf, -1, keepdims=True) + LN_EPS)


def gdn_inputs(qkv, gb, conv_w, a_log, dt_bias):
    b, n, _ = qkv.shape
    qkv = jax.nn.silu(centred_conv(qkv, conv_w))
    q, k, v = jnp.split(qkv, 3, axis=-1)
    q = l2norm(q.reshape(b, n, DN_HEADS, DN_HEAD_DIM)) * DN_HEAD_DIM ** -0.5
    k = l2norm(k.reshape(b, n, DN_HEADS, DN_HEAD_DIM))
    v = v.reshape(b, n, DN_HEADS, DN_HEAD_DIM)
    gb = gb.astype(jnp.float32)
    beta = jax.nn.sigmoid(gb[..., :2 * DN_HEADS]).reshape(b, n, 2, DN_HEADS)
    a = gb[..., 2 * DN_HEADS:].reshape(b, n, 2, DN_HEADS)
    g = -jnp.exp(a_log.astype(jnp.float32)) * jax.nn.softplus(a + dt_bias.astype(jnp.float32))
    return q, k, v, g, beta


def gated_delta_rule(q, k, v, g, beta, s0):
    b, n, h, dk = q.shape
    dv = v.shape[-1]
    nc = n // DN_CHUNK

    def chunks(t):
        t = t.astype(jnp.float32).reshape((b, nc, DN_CHUNK) + t.shape[2:])
        return jnp.swapaxes(t, 2, 3)

    q, k, v, g, beta = chunks(q), chunks(k), chunks(v), chunks(g), chunks(beta)
    gam = jnp.cumsum(g, axis=-1)
    causal = jnp.tril(jnp.ones((DN_CHUNK, DN_CHUNK), bool))
    strict = jnp.tril(jnp.ones((DN_CHUNK, DN_CHUNK), bool), -1)
    diff = gam[..., :, None] - gam[..., None, :]
    decay = jnp.where(causal, jnp.exp(jnp.where(causal, diff, 0.0)), 0.0)
    kb = k * beta[..., None]
    a_mat = jnp.where(strict, jnp.einsum('bnhid,bnhjd->bnhij', kb, k) * decay, 0.0)
    rhs = jnp.concatenate([v * beta[..., None], kb * jnp.exp(gam)[..., None]], -1)
    sol = lax.linalg.triangular_solve(a_mat, rhs, left_side=True, lower=True, unit_diagonal=True)
    u_c, w_c = sol[..., :dv], sol[..., dv:]
    qk = jnp.einsum('bnhid,bnhjd->bnhij', q, k) * decay
    q_dec = q * jnp.exp(gam)[..., None]
    k_dec = k * jnp.exp(gam[..., -1:] - gam)[..., None]
    g_tot = jnp.exp(gam[..., -1])

    def step(state, inp):
        qk_i, qd_i, kd_i, u_i, w_i, gt_i = inp
        v_new = u_i - jnp.einsum('bhid,bhde->bhie', w_i, state)
        o = jnp.einsum('bhid,bhde->bhie', qd_i, state) + jnp.einsum('bhij,bhje->bhie', qk_i, v_new)
        state = state * gt_i[..., None, None] + jnp.einsum('bhid,bhie->bhde', kd_i, v_new)
        return state, o

    xs = tuple(jnp.moveaxis(t, 1, 0) for t in (qk, q_dec, k_dec, u_c, w_c, g_tot))
    s_final, o = lax.scan(step, s0, xs)
    o = jnp.swapaxes(jnp.moveaxis(o, 0, 1), 2, 3).reshape(b, n, h, dv)
    return o, s_final


def bidir_delta(q, k, v, g, beta, s0_f, s0_b):
    o_f, s_f = gated_delta_rule(q, k, v, g[:, :, 0], beta[:, :, 0], s0_f)
    flip = lambda t: jnp.flip(t, axis=1)
    o_b, s_b = gated_delta_rule(flip(q), flip(k), flip(v), flip(g[:, :, 1]), flip(beta[:, :, 1]), s0_b)
    return o_f + flip(o_b), s_f, s_b


def gated_rmsnorm(o, z, w):
    b, n = z.shape[:2]
    y = o * lax.rsqrt(jnp.mean(o * o, -1, keepdims=True) + LN_EPS) * w.astype(jnp.float32)
    return y.astype(z.dtype).reshape(b, n, DN_W) * jax.nn.silu(z)


def mixing_sublayer(h, hc, w_in, conv_w, a_log, dt_bias, sink, dn_norm_w, w_out, with_ctx_out):
    b, n, _ = h.shape
    m = hc.shape[1]
    rows = n // GRID_W
    row = jnp.broadcast_to(jnp.arange(rows)[:, None], (rows, GRID_W)).reshape(n)
    col = jnp.broadcast_to(jnp.arange(GRID_W)[None, :], (rows, GRID_W)).reshape(n)
    qa, ka, va, qkv_d, z, gb = jnp.split(h @ w_in, IN_SPLITS, axis=-1)
    qa_c, ka_c, va_c, qkv_dc, z_c, gb_c = jnp.split(hc @ w_in, IN_SPLITS, axis=-1)
    qa = axial_rope(qa.reshape(b, n, ATTN_HEADS, ATTN_HEAD_DIM), row, col)
    ka = axial_rope(ka.reshape(b, n, ATTN_KV_HEADS, ATTN_HEAD_DIM), row, col)
    va = va.reshape(b, n, ATTN_KV_HEADS, ATTN_HEAD_DIM)
    ka_c = ka_c.reshape(b, m, ATTN_KV_HEADS, ATTN_HEAD_DIM)
    va_c = va_c.reshape(b, m, ATTN_KV_HEADS, ATTN_HEAD_DIM)
    attn = banded_window_attention(qa, ka, va, ka_c, va_c, sink)
    qd_c, kd_c, vd_c, g_c, beta_c = gdn_inputs(qkv_dc, gb_c, conv_w, a_log, dt_bias)
    s0 = jnp.zeros((b, DN_HEADS, DN_HEAD_DIM, DN_HEAD_DIM), jnp.float32)
    o_c, s_f, s_b = bidir_delta(qd_c, kd_c, vd_c, g_c, beta_c, s0, s0)
    qd, kd, vd, g, beta = gdn_inputs(qkv_d, gb, conv_w, a_log, dt_bias)
    o, _, _ = bidir_delta(qd, kd, vd, g, beta, s_f, s_b)
    y = jnp.concatenate([attn, gated_rmsnorm(o, z, dn_norm_w)], -1) @ w_out
    if not with_ctx_out:
        return y, None
    attn_c = context_attention(qa_c.reshape(b, m, ATTN_HEADS, ATTN_HEAD_DIM), ka_c, va_c, sink)
    y_c = jnp.concatenate([attn_c, gated_rmsnorm(o_c, z_c, dn_norm_w)], -1) @ w_out
    return y, y_c


def peer(h, w_query, sub_keys, u_tab, v_tab):
    b, n, d = h.shape
    tokens = h.reshape(-1, PEER_BLOCK, d)

    def one_block(xt):
        q = (xt @ w_query).reshape(PEER_BLOCK, PEER_HEADS, 2, PEER_KEY_HALF)
        s = jnp.einsum('phxd,xkd->phxk', q, sub_keys).astype(jnp.float32)
        s_top, i_top = lax.top_k(s, PEER_TOPK)
        cand = (s_top[:, :, 0, :, None] + s_top[:, :, 1, None, :]).reshape(
            PEER_BLOCK, PEER_HEADS, PEER_TOPK * PEER_TOPK)
        best, pos = lax.top_k(cand, PEER_TOPK)
        i1 = jnp.take_along_axis(i_top[:, :, 0], pos // PEER_TOPK, axis=-1)
        i2 = jnp.take_along_axis(i_top[:, :, 1], pos % PEER_TOPK, axis=-1)
        expert = i1 * PEER_N_KEYS + i2
        gate = jax.nn.softmax(best, axis=-1).astype(xt.dtype)
        act = jax.nn.gelu(jnp.einsum('phkd,pd->phk', u_tab[expert], xt), approximate=False)
        return jnp.einsum('phk,phkd->pd', gate * act, v_tab[expert])

    return lax.map(one_block, tokens).reshape(b, n, d)


def setup_inputs(seed: int = 0) -> dict:
    key = jax.random.key(seed)
    ks = jax.random.split(key, 22)
    d = D_MODEL

    def nrm(k, shape, s):
        return jax.random.normal(k, shape, jnp.float32) * s

    dt = jnp.exp(jax.random.uniform(ks[9], (DEPTH, 2, DN_HEADS), jnp.float32,
                                    minval=math.log(1e-3), maxval=math.log(1e-1)))
    return {
        "x": nrm(ks[0], (BATCH, SEQ, d), 1.0),
        "c": nrm(ks[1], (BATCH, d), 1.0),
        "ctx": nrm(ks[2], (BATCH, CTX_LEN, d), 1.0),
        "c_ctx": nrm(ks[3], (d,), 1.0),
        "w_ada": nrm(ks[4], (DEPTH, d, 6 * d), d ** -0.5),
        "b_ada": nrm(ks[5], (DEPTH, 6 * d), 0.02),
        "w_in": nrm(ks[6], (DEPTH, d, D_IN), d ** -0.5),
        "conv_w": nrm(ks[7], (DEPTH, DN_CONV, 3 * DN_W), DN_CONV ** -0.5),
        "a_log": jnp.log(jax.random.uniform(ks[8], (DEPTH, 2, DN_HEADS), jnp.float32,
                                            minval=1.0, maxval=16.0)),
        "dt_bias": dt + jnp.log(-jnp.expm1(-dt)),
        "sink": nrm(ks[10], (DEPTH, ATTN_HEADS), 0.5),
        "dn_norm_w": 1.0 + nrm(ks[11], (DEPTH, DN_HEAD_DIM), 0.02),
        "w_out": nrm(ks[12], (DEPTH, MIX_W, d), MIX_W ** -0.5 * DEEPNORM_BETA),
        "ln1_g": 1.0 + nrm(ks[13], (DEPTH, d), 0.02),
        "ln1_b": nrm(ks[14], (DEPTH, d), 0.02),
        "peer_wq": nrm(ks[15], (DEPTH, d, PEER_HEADS * 2 * PEER_KEY_HALF), d ** -0.5),
        "peer_sub_keys": nrm(ks[16], (DEPTH, 2, PEER_N_KEYS, PEER_KEY_HALF), PEER_KEY_HALF ** -0.5),
        "peer_u": nrm(ks[17], (DEPTH, PEER_EXPERTS, d), d ** -0.5),
        "peer_v": nrm(ks[18], (DEPTH, PEER_EXPERTS, d), DEEPNORM_BETA),
        "ln2_g": 1.0 + nrm(ks[19], (DEPTH, d), 0.02),
        "ln2_b": nrm(ks[20], (DEPTH, d), 0.02),
    }


def reference(x, c, ctx, c_ctx, w_ada, b_ada, w_in, conv_w, a_log, dt_bias, sink, dn_norm_w, w_out,
              ln1_g, ln1_b, peer_wq, peer_sub_keys, peer_u, peer_v, ln2_g, ln2_b):
    for l in range(DEPTH):
        last = l == DEPTH - 1
        mod = jax.nn.silu(c) @ w_ada[l] + b_ada[l]
        mod_c = jax.nn.silu(c_ctx) @ w_ada[l] + b_ada[l]
        sh1, sc1, gt1, sh2, sc2, gt2 = [t[:, None, :] for t in jnp.split(mod, 6, axis=-1)]
        csh1, csc1, cgt1, csh2, csc2, cgt2 = jnp.split(mod_c, 6, axis=-1)
        h = modulate(x, sh1, sc1)
        hc = modulate(ctx, csh1, csc1)
        y, y_c = mixing_sublayer(h, hc, w_in[l], conv_w[l], a_log[l], dt_bias[l], sink[l],
                                 dn_norm_w[l], w_out[l], not last)
        x = layer_norm(DEEPNORM_ALPHA * x + gt1 * y, ln1_g[l], ln1_b[l])
        h = modulate(x, sh2, sc2)
        x = layer_norm(DEEPNORM_ALPHA * x + gt2 * peer(h, peer_wq[l], peer_sub_keys[l], peer_u[l], peer_v[l]),
                       ln2_g[l], ln2_b[l])
        if not last:
            ctx = layer_norm(DEEPNORM_ALPHA * ctx + cgt1 * y_c, ln1_g[l], ln1_b[l])
            hc = modulate(ctx, csh2, csc2)
            ctx = layer_norm(DEEPNORM_ALPHA * ctx + cgt2 * peer(hc, peer_wq[l], peer_sub_keys[l], peer_u[l], peer_v[l]),
                             ln2_g[l], ln2_b[l])
    return x
```

```python
import math
from functools import partial

import jax
import jax.numpy as jnp
from jax import lax
from jax.experimental import pallas as pl
from jax.experimental.pallas import tpu as pltpu

D_MODEL = 1024
BATCH = 4
SEQ = 8192
DEPTH = 1

GRID_W = 64
CTX_LEN = 256

ATTN_HEADS = 8
ATTN_KV_HEADS = 2
ATTN_GROUP = ATTN_HEADS // ATTN_KV_HEADS
ATTN_HEAD_DIM = 64
WINDOW = 128
ATTN_BLOCK = 128
ROPE_BASE = 10000.0
DN_HEADS = 4
DN_HEAD_DIM = 128
DN_CONV = 5
DN_CHUNK = 64
ATTN_Q_W = ATTN_HEADS * ATTN_HEAD_DIM
ATTN_KV_W = ATTN_KV_HEADS * ATTN_HEAD_DIM
DN_W = DN_HEADS * DN_HEAD_DIM
MIX_W = ATTN_Q_W + DN_W
D_IN = ATTN_Q_W + 2 * ATTN_KV_W + 4 * DN_W + 4 * DN_HEADS
IN_SPLITS = (ATTN_Q_W, ATTN_Q_W + ATTN_KV_W, ATTN_Q_W + 2 * ATTN_KV_W,
             ATTN_Q_W + 2 * ATTN_KV_W + 3 * DN_W, ATTN_Q_W + 2 * ATTN_KV_W + 4 * DN_W)
PEER_HEADS = 8
PEER_N_KEYS = 128
PEER_EXPERTS = PEER_N_KEYS ** 2
PEER_KEY_HALF = 128
PEER_TOPK = 16
PEER_BLOCK = 128
DEEPNORM_ALPHA = (2 * DEPTH) ** 0.25
DEEPNORM_BETA = (8 * DEPTH) ** -0.25
LN_EPS = 1e-6
NEG_INF = -1e30


def standardize(x):
    xf = x.astype(jnp.float32)
    xc = xf - xf.mean(-1, keepdims=True)
    var = (xc * xc).mean(-1, keepdims=True)
    return (xc * lax.rsqrt(var + LN_EPS)).astype(x.dtype)


def layer_norm(x, g, b):
    return standardize(x) * g + b


def modulate(x, shift, scale):
    return standardize(x) * (1 + scale) + shift


def axial_rope(x, row, col):
    half = x.shape[-1] // 2
    nf = half // 2
    freqs = ROPE_BASE ** (-jnp.arange(nf, dtype=jnp.float32) / nf)

    def rotate(xa, pos):
        ang = pos.astype(jnp.float32)[:, None] * freqs
        cos = jnp.cos(ang)[None, :, None, :].astype(x.dtype)
        sin = jnp.sin(ang)[None, :, None, :].astype(x.dtype)
        x1, x2 = xa[..., :nf], xa[..., nf:]
        return jnp.concatenate([x1 * cos - x2 * sin, x1 * sin + x2 * cos], -1)

    return jnp.concatenate([rotate(x[..., :half], row), rotate(x[..., half:], col)], -1)


def softmax_with_sink(logits, sink):
    snk = jnp.broadcast_to(sink.astype(jnp.float32).reshape(ATTN_KV_HEADS, ATTN_GROUP, 1, 1),
                           logits.shape[:-1] + (1,))
    return jax.nn.softmax(jnp.concatenate([logits, snk], -1), axis=-1)[..., :-1]


def banded_window_attention(q, k, v, k_ctx, v_ctx, sink):
    b, n = q.shape[:2]
    nblk = n // ATTN_BLOCK
    scale = ATTN_HEAD_DIM ** -0.5
    qb = q.reshape(b, nblk, ATTN_BLOCK, ATTN_KV_HEADS, ATTN_GROUP, ATTN_HEAD_DIM)
    pad = ((0, 0), (ATTN_BLOCK, ATTN_BLOCK), (0, 0), (0, 0))
    k_pad = jnp.pad(k, pad)
    v_pad = jnp.pad(v, pad)
    n_loc = 3 * ATTN_BLOCK
    q_off = jnp.arange(ATTN_BLOCK)
    k_off = jnp.arange(n_loc) - ATTN_BLOCK
    in_band = jnp.abs(q_off[:, None] - k_off[None, :]) <= WINDOW

    def one_block(i):
        qi = lax.dynamic_index_in_dim(qb, i, axis=1, keepdims=False)
        ki = lax.dynamic_slice_in_dim(k_pad, i * ATTN_BLOCK, n_loc, axis=1)
        vi = lax.dynamic_slice_in_dim(v_pad, i * ATTN_BLOCK, n_loc, axis=1)
        k_abs = i * ATTN_BLOCK + k_off
        mask = in_band & ((k_abs >= 0) & (k_abs < n))[None, :]
        s_loc = jnp.einsum('bqhgd,bkhd->bhgqk', qi, ki).astype(jnp.float32) * scale
        s_loc = jnp.where(mask, s_loc, NEG_INF)
        s_ctx = jnp.einsum('bqhgd,bkhd->bhgqk', qi, k_ctx).astype(jnp.float32) * scale
        p = softmax_with_sink(jnp.concatenate([s_loc, s_ctx], -1), sink).astype(v.dtype)
        return (jnp.einsum('bhgqk,bkhd->bqhgd', p[..., :n_loc], vi)
                + jnp.einsum('bhgqk,bkhd->bqhgd', p[..., n_loc:], v_ctx))

    o = lax.map(one_block, jnp.arange(nblk))
    return jnp.moveaxis(o, 0, 1).reshape(b, n, ATTN_Q_W)


def centred_conv(x, w):
    pad = (w.shape[0] - 1) // 2
    return lax.conv_general_dilated(x, w[:, None, :], window_strides=(1,), padding=[(pad, pad)],
                                    dimension_numbers=('NWC', 'WIO', 'NWC'),
                                    feature_group_count=x.shape[-1])


def l2norm(x):
    xf = x.astype(jnp.float32)
    return xf * lax.rsqrt(jnp.sum(xf * xf, -1, keepdims=True) + LN_EPS)


def gdn_inputs(qkv, gb, conv_w, a_log, dt_bias):
    b, n, _ = qkv.shape
    qkv = jax.nn.silu(centred_conv(qkv, conv_w))
    q, k, v = jnp.split(qkv, 3, axis=-1)
    q = l2norm(q.reshape(b, n, DN_HEADS, DN_HEAD_DIM)) * DN_HEAD_DIM ** -0.5
    k = l2norm(k.reshape(b, n, DN_HEADS, DN_HEAD_DIM))
    v = v.reshape(b, n, DN_HEADS, DN_HEAD_DIM)
    gb = gb.astype(jnp.float32)
    beta = jax.nn.sigmoid(gb[..., :2 * DN_HEADS]).reshape(b, n, 2, DN_HEADS)
    a = gb[..., 2 * DN_HEADS:].reshape(b, n, 2, DN_HEADS)
    g = -jnp.exp(a_log.astype(jnp.float32)) * jax.nn.softplus(a + dt_bias.astype(jnp.float32))
    return q, k, v, g, beta


def gated_delta_rule(q, k, v, g, beta, s0):
    b, n, h, dk = q.shape
    dv = v.shape[-1]
    nc = n // DN_CHUNK

    def chunks(t):
        t = t.astype(jnp.float32).reshape((b, nc, DN_CHUNK) + t.shape[2:])
        return jnp.swapaxes(t, 2, 3)

    q, k, v, g, beta = chunks(q), chunks(k), chunks(v), chunks(g), chunks(beta)
    gam = jnp.cumsum(g, axis=-1)
    causal = jnp.tril(jnp.ones((DN_CHUNK, DN_CHUNK), bool))
    strict = jnp.tril(jnp.ones((DN_CHUNK, DN_CHUNK), bool), -1)
    diff = gam[..., :, None] - gam[..., None, :]
    decay = jnp.where(causal, jnp.exp(jnp.where(causal, diff, 0.0)), 0.0)
    kb = k * beta[..., None]
    a_mat = jnp.where(strict, jnp.einsum('bnhid,bnhjd->bnhij', kb, k) * decay, 0.0)
    rhs = jnp.concatenate([v * beta[..., None], kb * jnp.exp(gam)[..., None]], -1)
    sol = lax.linalg.triangular_solve(a_mat, rhs, left_side=True, lower=True, unit_diagonal=True)
    u_c, w_c = sol[..., :dv], sol[..., dv:]
    qk = jnp.einsum('bnhid,bnhjd->bnhij', q, k) * decay
    q_dec = q * jnp.exp(gam)[..., None]
    k_dec = k * jnp.exp(gam[..., -1:] - gam)[..., None]
    g_tot = jnp.exp(gam[..., -1])

    def step(state, inp):
        qk_i, qd_i, kd_i, u_i, w_i, gt_i = inp
        v_new = u_i - jnp.einsum('bhid,bhde->bhie', w_i, state)
        o = jnp.einsum('bhid,bhde->bhie', qd_i, state) + jnp.einsum('bhij,bhje->bhie', qk_i, v_new)
        state = state * gt_i[..., None, None] + jnp.einsum('bhid,bhie->bhde', kd_i, v_new)
        return state, o

    xs = tuple(jnp.moveaxis(t, 1, 0) for t in (qk, q_dec, k_dec, u_c, w_c, g_tot))
    s_final, o = lax.scan(step, s0, xs)
    o = jnp.swapaxes(jnp.moveaxis(o, 0, 1), 2, 3).reshape(b, n, h, dv)
    return o, s_final


def bidir_delta(q, k, v, g, beta, s0_f, s0_b):
    o_f, s_f = gated_delta_rule(q, k, v, g[:, :, 0], beta[:, :, 0], s0_f)
    flip = lambda t: jnp.flip(t, axis=1)
    o_b, s_b = gated_delta_rule(flip(q), flip(k), flip(v), flip(g[:, :, 1]), flip(beta[:, :, 1]), s0_b)
    return o_f + flip(o_b), s_f, s_b


def gated_rmsnorm(o, z, w):
    b, n = z.shape[:2]
    y = o * lax.rsqrt(jnp.mean(o * o, -1, keepdims=True) + LN_EPS) * w.astype(jnp.float32)
    return y.astype(z.dtype).reshape(b, n, DN_W) * jax.nn.silu(z)


def mixing_sublayer(h, hc, w_in, conv_w, a_log, dt_bias, sink, dn_norm_w, w_out):
    b, n, _ = h.shape
    m = hc.shape[1]
    rows = n // GRID_W
    row = jnp.broadcast_to(jnp.arange(rows)[:, None], (rows, GRID_W)).reshape(n)
    col = jnp.broadcast_to(jnp.arange(GRID_W)[None, :], (rows, GRID_W)).reshape(n)
    qa, ka, va, qkv_d, z, gb = jnp.split(h @ w_in, IN_SPLITS, axis=-1)
    qa_c, ka_c, va_c, qkv_dc, z_c, gb_c = jnp.split(hc @ w_in, IN_SPLITS, axis=-1)
    qa = axial_rope(qa.reshape(b, n, ATTN_HEADS, ATTN_HEAD_DIM), row, col)
    ka = axial_rope(ka.reshape(b, n, ATTN_KV_HEADS, ATTN_HEAD_DIM), row, col)
    va = va.reshape(b, n, ATTN_KV_HEADS, ATTN_HEAD_DIM)
    ka_c = ka_c.reshape(b, m, ATTN_KV_HEADS, ATTN_HEAD_DIM)
    va_c = va_c.reshape(b, m, ATTN_KV_HEADS, ATTN_HEAD_DIM)
    attn = banded_window_attention(qa, ka, va, ka_c, va_c, sink)
    qd_c, kd_c, vd_c, g_c, beta_c = gdn_inputs(qkv_dc, gb_c, conv_w, a_log, dt_bias)
    s0 = jnp.zeros((b, DN_HEADS, DN_HEAD_DIM, DN_HEAD_DIM), jnp.float32)
    o_c, s_f, s_b = bidir_delta(qd_c, kd_c, vd_c, g_c, beta_c, s0, s0)
    qd, kd, vd, g, beta = gdn_inputs(qkv_d, gb, conv_w, a_log, dt_bias)
    o, _, _ = bidir_delta(qd, kd, vd, g, beta, s_f, s_b)
    y = jnp.concatenate([attn, gated_rmsnorm(o, z, dn_norm_w)], -1) @ w_out
    return y


def peer(h, w_query, sub_keys, u_tab, v_tab):
    b, n, d = h.shape
    tokens = h.reshape(-1, PEER_BLOCK, d)

    def one_block(xt):
        q = (xt @ w_query).reshape(PEER_BLOCK, PEER_HEADS, 2, PEER_KEY_HALF)
        s = jnp.einsum('phxd,xkd->phxk', q, sub_keys).astype(jnp.float32)
        s_top, i_top = lax.top_k(s, PEER_TOPK)
        cand = (s_top[:, :, 0, :, None] + s_top[:, :, 1, None, :]).reshape(
            PEER_BLOCK, PEER_HEADS, PEER_TOPK * PEER_TOPK)
        best, pos = lax.top_k(cand, PEER_TOPK)
        i1 = jnp.take_along_axis(i_top[:, :, 0], pos // PEER_TOPK, axis=-1)
        i2 = jnp.take_along_axis(i_top[:, :, 1], pos % PEER_TOPK, axis=-1)
        expert = i1 * PEER_N_KEYS + i2
        gate = jax.nn.softmax(best, axis=-1).astype(xt.dtype)
        act = jax.nn.gelu(jnp.einsum('phkd,pd->phk', u_tab[expert], xt), approximate=False)
        return jnp.einsum('phk,phkd->pd', gate * act, v_tab[expert])

    return lax.map(one_block, tokens).reshape(b, n, d)


LN_ROWS = 512


def _res_ln_kernel(x_ref, y_ref, gate_ref, g_ref, b_ref, o_ref):
    x = x_ref[0]
    y = y_ref[0]
    t = DEEPNORM_ALPHA * x + gate_ref[0] * y
    mu = jnp.mean(t, axis=-1, keepdims=True)
    tc = t - mu
    var = jnp.mean(tc * tc, axis=-1, keepdims=True)
    o_ref[0] = tc * lax.rsqrt(var + LN_EPS) * g_ref[...] + b_ref[...]


def res_ln(x, y, gate, g, b):
    bsz, n, d = x.shape
    return pl.pallas_call(
        _res_ln_kernel,
        grid=(bsz, n // LN_ROWS),
        in_specs=[
            pl.BlockSpec((1, LN_ROWS, d), lambda i, j: (i, j, 0)),
            pl.BlockSpec((1, LN_ROWS, d), lambda i, j: (i, j, 0)),
            pl.BlockSpec((1, 1, d), lambda i, j: (i, 0, 0)),
            pl.BlockSpec((1, d), lambda i, j: (0, 0)),
            pl.BlockSpec((1, d), lambda i, j: (0, 0)),
        ],
        out_specs=pl.BlockSpec((1, LN_ROWS, d), lambda i, j: (i, j, 0)),
        out_shape=jax.ShapeDtypeStruct(x.shape, x.dtype),
        name="res_ln",
    )(x, y, gate.reshape(bsz, 1, d), g.reshape(1, d), b.reshape(1, d))


def kernel(x, c, ctx, c_ctx, w_ada, b_ada, w_in, conv_w, a_log, dt_bias, sink, dn_norm_w, w_out,
           ln1_g, ln1_b, peer_wq, peer_sub_keys, peer_u, peer_v, ln2_g, ln2_b):
    l = 0
    mod = jax.nn.silu(c) @ w_ada[l] + b_ada[l]
    mod_c = jax.nn.silu(c_ctx) @ w_ada[l] + b_ada[l]
    sh1, sc1, gt1, sh2, sc2, gt2 = [t[:, None, :] for t in jnp.split(mod, 6, axis=-1)]
    csh1, csc1, cgt1, csh2, csc2, cgt2 = jnp.split(mod_c, 6, axis=-1)
    h = modulate(x, sh1, sc1)
    hc = modulate(ctx, csh1, csc1)
    y = mixing_sublayer(h, hc, w_in[l], conv_w[l], a_log[l], dt_bias[l], sink[l],
                        dn_norm_w[l], w_out[l])
    x = res_ln(x, y, gt1[:, 0], ln1_g[l], ln1_b[l])
    h = modulate(x, sh2, sc2)
    p = peer(h, peer_wq[l], peer_sub_keys[l], peer_u[l], peer_v[l])
    x = res_ln(x, p, gt2[:, 0], ln2_g[l], ln2_b[l])
    return x
```

```python
import math
from functools import partial

import jax
import jax.numpy as jnp
from jax import lax
from jax.experimental import pallas as pl
from jax.experimental.pallas import tpu as pltpu

D_MODEL = 1024
BATCH = 4
SEQ = 8192
DEPTH = 1

GRID_W = 64
CTX_LEN = 256

ATTN_HEADS = 8
ATTN_KV_HEADS = 2
ATTN_GROUP = ATTN_HEADS // ATTN_KV_HEADS
ATTN_HEAD_DIM = 64
WINDOW = 128
ATTN_BLOCK = 128
ROPE_BASE = 10000.0
DN_HEADS = 4
DN_HEAD_DIM = 128
DN_CONV = 5
DN_CHUNK = 64
ATTN_Q_W = ATTN_HEADS * ATTN_HEAD_DIM
ATTN_KV_W = ATTN_KV_HEADS * ATTN_HEAD_DIM
DN_W = DN_HEADS * DN_HEAD_DIM
MIX_W = ATTN_Q_W + DN_W
D_IN = ATTN_Q_W + 2 * ATTN_KV_W + 4 * DN_W + 4 * DN_HEADS
IN_SPLITS = (ATTN_Q_W, ATTN_Q_W + ATTN_KV_W, ATTN_Q_W + 2 * ATTN_KV_W,
             ATTN_Q_W + 2 * ATTN_KV_W + 3 * DN_W, ATTN_Q_W + 2 * ATTN_KV_W + 4 * DN_W)
PEER_HEADS = 8
PEER_N_KEYS = 128
PEER_EXPERTS = PEER_N_KEYS ** 2
PEER_KEY_HALF = 128
PEER_TOPK = 16
PEER_BLOCK = 128
DEEPNORM_ALPHA = (2 * DEPTH) ** 0.25
DEEPNORM_BETA = (8 * DEPTH) ** -0.25
LN_EPS = 1e-6
NEG_INF = -1e30


def standardize(x):
    xf = x.astype(jnp.float32)
    xc = xf - xf.mean(-1, keepdims=True)
    var = (xc * xc).mean(-1, keepdims=True)
    return (xc * lax.rsqrt(var + LN_EPS)).astype(x.dtype)


def layer_norm(x, g, b):
    return standardize(x) * g + b


def modulate(x, shift, scale):
    return standardize(x) * (1 + scale) + shift


def axial_rope(x, row, col):
    half = x.shape[-1] // 2
    nf = half // 2
    freqs = ROPE_BASE ** (-jnp.arange(nf, dtype=jnp.float32) / nf)

    def rotate(xa, pos):
        ang = pos.astype(jnp.float32)[:, None] * freqs
        cos = jnp.cos(ang)[None, :, None, :].astype(x.dtype)
        sin = jnp.sin(ang)[None, :, None, :].astype(x.dtype)
        x1, x2 = xa[..., :nf], xa[..., nf:]
        return jnp.concatenate([x1 * cos - x2 * sin, x1 * sin + x2 * cos], -1)

    return jnp.concatenate([rotate(x[..., :half], row), rotate(x[..., half:], col)], -1)


def softmax_with_sink(logits, sink):
    snk = jnp.broadcast_to(sink.astype(jnp.float32).reshape(ATTN_KV_HEADS, ATTN_GROUP, 1, 1),
                           logits.shape[:-1] + (1,))
    return jax.nn.softmax(jnp.concatenate([logits, snk], -1), axis=-1)[..., :-1]


def banded_window_attention(q, k, v, k_ctx, v_ctx, sink):
    b, n = q.shape[:2]
    nblk = n // ATTN_BLOCK
    scale = ATTN_HEAD_DIM ** -0.5
    qb = q.reshape(b, nblk, ATTN_BLOCK, ATTN_KV_HEADS, ATTN_GROUP, ATTN_HEAD_DIM)
    pad = ((0, 0), (ATTN_BLOCK, ATTN_BLOCK), (0, 0), (0, 0))
    k_pad = jnp.pad(k, pad)
    v_pad = jnp.pad(v, pad)
    n_loc = 3 * ATTN_BLOCK
    q_off = jnp.arange(ATTN_BLOCK)
    k_off = jnp.arange(n_loc) - ATTN_BLOCK
    in_band = jnp.abs(q_off[:, None] - k_off[None, :]) <= WINDOW

    def one_block(i):
        qi = lax.dynamic_index_in_dim(qb, i, axis=1, keepdims=False)
        ki = lax.dynamic_slice_in_dim(k_pad, i * ATTN_BLOCK, n_loc, axis=1)
        vi = lax.dynamic_slice_in_dim(v_pad, i * ATTN_BLOCK, n_loc, axis=1)
        k_abs = i * ATTN_BLOCK + k_off
        mask = in_band & ((k_abs >= 0) & (k_abs < n))[None, :]
        s_loc = jnp.einsum('bqhgd,bkhd->bhgqk', qi, ki).astype(jnp.float32) * scale
        s_loc = jnp.where(mask, s_loc, NEG_INF)
        s_ctx = jnp.einsum('bqhgd,bkhd->bhgqk', qi, k_ctx).astype(jnp.float32) * scale
        p = softmax_with_sink(jnp.concatenate([s_loc, s_ctx], -1), sink).astype(v.dtype)
        return (jnp.einsum('bhgqk,bkhd->bqhgd', p[..., :n_loc], vi)
                + jnp.einsum('bhgqk,bkhd->bqhgd', p[..., n_loc:], v_ctx))

    o = lax.map(one_block, jnp.arange(nblk))
    return jnp.moveaxis(o, 0, 1).reshape(b, n, ATTN_Q_W)


def centred_conv(x, w):
    pad = (w.shape[0] - 1) // 2
    return lax.conv_general_dilated(x, w[:, None, :], window_strides=(1,), padding=[(pad, pad)],
                                    dimension_numbers=('NWC', 'WIO', 'NWC'),
                                    feature_group_count=x.shape[-1])


def l2norm(x):
    xf = x.astype(jnp.float32)
    return xf * lax.rsqrt(jnp.sum(xf * xf, -1, keepdims=True) + LN_EPS)


def gdn_inputs(qkv, gb, conv_w, a_log, dt_bias):
    b, n, _ = qkv.shape
    qkv = jax.nn.silu(centred_conv(qkv, conv_w))
    q, k, v = jnp.split(qkv, 3, axis=-1)
    q = l2norm(q.reshape(b, n, DN_HEADS, DN_HEAD_DIM)) * DN_HEAD_DIM ** -0.5
    k = l2norm(k.reshape(b, n, DN_HEADS, DN_HEAD_DIM))
    v = v.reshape(b, n, DN_HEADS, DN_HEAD_DIM)
    gb = gb.astype(jnp.float32)
    beta = jax.nn.sigmoid(gb[..., :2 * DN_HEADS]).reshape(b, n, 2, DN_HEADS)
    a = gb[..., 2 * DN_HEADS:].reshape(b, n, 2, DN_HEADS)
    g = -jnp.exp(a_log.astype(jnp.float32)) * jax.nn.softplus(a + dt_bias.astype(jnp.float32))
    return q, k, v, g, beta


def gated_delta_rule(q, k, v, g, beta, s0):
    b, n, h, dk = q.shape
    dv = v.shape[-1]
    nc = n // DN_CHUNK

    def chunks(t):
        t = t.astype(jnp.float32).reshape((b, nc, DN_CHUNK) + t.shape[2:])
        return jnp.swapaxes(t, 2, 3)

    q, k, v, g, beta = chunks(q), chunks(k), chunks(v), chunks(g), chunks(beta)
    gam = jnp.cumsum(g, axis=-1)
    causal = jnp.tril(jnp.ones((DN_CHUNK, DN_CHUNK), bool))
    strict = jnp.tril(jnp.ones((DN_CHUNK, DN_CHUNK), bool), -1)
    diff = gam[..., :, None] - gam[..., None, :]
    decay = jnp.where(causal, jnp.exp(jnp.where(causal, diff, 0.0)), 0.0)
    kb = k * beta[..., None]
    a_mat = jnp.where(strict, jnp.einsum('bnhid,bnhjd->bnhij', kb, k) * decay, 0.0)
    rhs = jnp.concatenate([v * beta[..., None], kb * jnp.exp(gam)[..., None]], -1)
    sol = lax.linalg.triangular_solve(a_mat, rhs, left_side=True, lower=True, unit_diagonal=True)
    u_c, w_c = sol[..., :dv], sol[..., dv:]
    qk = jnp.einsum('bnhid,bnhjd->bnhij', q, k) * decay
    q_dec = q * jnp.exp(gam)[..., None]
    k_dec = k * jnp.exp(gam[..., -1:] - gam)[..., None]
    g_tot = jnp.exp(gam[..., -1])

    def step(state, inp):
        qk_i, qd_i, kd_i, u_i, w_i, gt_i = inp
        v_new = u_i - jnp.einsum('bhid,bhde->bhie', w_i, state)
        o = jnp.einsum('bhid,bhde->bhie', qd_i, state) + jnp.einsum('bhij,bhje->bhie', qk_i, v_new)
        state = state * gt_i[..., None, None] + jnp.einsum('bhid,bhie->bhde', kd_i, v_new)
        return state, o

    xs = tuple(jnp.moveaxis(t, 1, 0) for t in (qk, q_dec, k_dec, u_c, w_c, g_tot))
    s_final, o = lax.scan(step, s0, xs)
    o = jnp.swapaxes(jnp.moveaxis(o, 0, 1), 2, 3).reshape(b, n, h, dv)
    return o, s_final


def bidir_delta(q, k, v, g, beta, s0_f, s0_b):
    o_f, s_f = gated_delta_rule(q, k, v, g[:, :, 0], beta[:, :, 0], s0_f)
    flip = lambda t: jnp.flip(t, axis=1)
    o_b, s_b = gated_delta_rule(flip(q), flip(k), flip(v), flip(g[:, :, 1]), flip(beta[:, :, 1]), s0_b)
    return o_f + flip(o_b), s_f, s_b


def gated_rmsnorm(o, z, w):
    b, n = z.shape[:2]
    y = o * lax.rsqrt(jnp.mean(o * o, -1, keepdims=True) + LN_EPS) * w.astype(jnp.float32)
    return y.astype(z.dtype).reshape(b, n, DN_W) * jax.nn.silu(z)


def mixing_sublayer(h, hc, w_in, conv_w, a_log, dt_bias, sink, dn_norm_w, w_out):
    b, n, _ = h.shape
    m = hc.shape[1]
    rows = n // GRID_W
    row = jnp.broadcast_to(jnp.arange(rows)[:, None], (rows, GRID_W)).reshape(n)
    col = jnp.broadcast_to(jnp.arange(GRID_W)[None, :], (rows, GRID_W)).reshape(n)
    qa, ka, va, qkv_d, z, gb = jnp.split(h @ w_in, IN_SPLITS, axis=-1)
    qa_c, ka_c, va_c, qkv_dc, z_c, gb_c = jnp.split(hc @ w_in, IN_SPLITS, axis=-1)
    qa = axial_rope(qa.reshape(b, n, ATTN_HEADS, ATTN_HEAD_DIM), row, col)
    ka = axial_rope(ka.reshape(b, n, ATTN_KV_HEADS, ATTN_HEAD_DIM), row, col)
    va = va.reshape(b, n, ATTN_KV_HEADS, ATTN_HEAD_DIM)
    ka_c = ka_c.reshape(b, m, ATTN_KV_HEADS, ATTN_HEAD_DIM)
    va_c = va_c.reshape(b, m, ATTN_KV_HEADS, ATTN_HEAD_DIM)
    attn = banded_window_attention(qa, ka, va, ka_c, va_c, sink)
    qd_c, kd_c, vd_c, g_c, beta_c = gdn_inputs(qkv_dc, gb_c, conv_w, a_log, dt_bias)
    s0 = jnp.zeros((b, DN_HEADS, DN_HEAD_DIM, DN_HEAD_DIM), jnp.float32)
    o_c, s_f, s_b = bidir_delta(qd_c, kd_c, vd_c, g_c, beta_c, s0, s0)
    qd, kd, vd, g, beta = gdn_inputs(qkv_d, gb, conv_w, a_log, dt_bias)
    o, _, _ = bidir_delta(qd, kd, vd, g, beta, s_f, s_b)
    y = jnp.concatenate([attn, gated_rmsnorm(o, z, dn_norm_w)], -1) @ w_out
    return y


def peer_select(h, w_query, sub_keys):
    b, n, d = h.shape
    tokens = h.reshape(-1, PEER_BLOCK, d)

    def one_block(xt):
        q = (xt @ w_query).reshape(PEER_BLOCK, PEER_HEADS, 2, PEER_KEY_HALF)
        s = jnp.einsum('phxd,xkd->phxk', q, sub_keys).astype(jnp.float32)
        s_top, i_top = lax.top_k(s, PEER_TOPK)
        cand = (s_top[:, :, 0, :, None] + s_top[:, :, 1, None, :]).reshape(
            PEER_BLOCK, PEER_HEADS, PEER_TOPK * PEER_TOPK)
        best, pos = lax.top_k(cand, PEER_TOPK)
        i1 = jnp.take_along_axis(i_top[:, :, 0], pos // PEER_TOPK, axis=-1)
        i2 = jnp.take_along_axis(i_top[:, :, 1], pos % PEER_TOPK, axis=-1)
        expert = i1 * PEER_N_KEYS + i2
        gate = jax.nn.softmax(best, axis=-1).astype(xt.dtype)
        return expert.reshape(PEER_BLOCK, PEER_SEL), gate.reshape(PEER_BLOCK, PEER_SEL)

    expert, gate = lax.map(one_block, tokens)
    return expert.reshape(b * n, PEER_SEL), gate.reshape(b * n, PEER_SEL)


PEER_SEL = PEER_HEADS * PEER_TOPK
PG_TOKENS = 128
PG_SLOTS = 8
_SQRT_HALF = 0.7071067811865476


def _peer_eval_kernel(idx_ref, x_ref, gt_ref, uv_hbm, o_ref, buf, sem):
    d = x_ref.shape[-1]

    def row_copy(t, e, slot):
        row = idx_ref[t * PEER_SEL + e]
        return pltpu.make_async_copy(uv_hbm.at[pl.ds(row, 1)], buf.at[slot, pl.ds(e, 1)], sem.at[slot])

    def issue(t):
        slot = t % PG_SLOTS
        for e in range(PEER_SEL):
            row_copy(t, e, slot).start()

    def wait(slot):
        pltpu.make_async_copy(uv_hbm.at[pl.ds(0, PEER_SEL)], buf.at[slot], sem.at[slot]).wait()

    for t in range(PG_SLOTS - 1):
        issue(t)

    lane = lax.broadcasted_iota(jnp.int32, (PEER_SEL, PG_TOKENS), 1)

    def step(t, carry):
        @pl.when(t + PG_SLOTS - 1 < PG_TOKENS)
        def _():
            issue(t + PG_SLOTS - 1)

        slot = t % PG_SLOTS
        wait(slot)
        x = x_ref[pl.ds(t, 1), :]
        a = jnp.sum(buf[slot, :, :d] * x, axis=-1, keepdims=True)
        act = 0.5 * a * (1.0 + lax.erf(a * _SQRT_HALF))
        g = jnp.sum(jnp.where(lane == t, gt_ref[...], 0.0), axis=-1, keepdims=True)
        o_ref[pl.ds(t, 1), :] = jnp.sum((g * act) * buf[slot, :, d:], axis=0, keepdims=True)
        return carry

    lax.fori_loop(0, PG_TOKENS, step, 0)


def peer_eval(x, expert, gate_t, uv):
    t, d = x.shape
    return pl.pallas_call(
        _peer_eval_kernel,
        grid=(t // PG_TOKENS,),
        in_specs=[
            pl.BlockSpec((PG_TOKENS * PEER_SEL,), lambda i: (i,), memory_space=pltpu.SMEM),
            pl.BlockSpec((PG_TOKENS, d), lambda i: (i, 0)),
            pl.BlockSpec((PEER_SEL, PG_TOKENS), lambda i: (0, i)),
            pl.BlockSpec(memory_space=pl.ANY),
        ],
        out_specs=pl.BlockSpec((PG_TOKENS, d), lambda i: (i, 0)),
        out_shape=jax.ShapeDtypeStruct((t, d), jnp.float32),
        scratch_shapes=[
            pltpu.VMEM((PG_SLOTS, PEER_SEL, 2 * d), jnp.float32),
            pltpu.SemaphoreType.DMA((PG_SLOTS,)),
        ],
        compiler_params=pltpu.CompilerParams(dimension_semantics=("arbitrary",)),
        name="peer_eval",
    )(expert, x, gate_t, uv)


def peer(h, w_query, sub_keys, u_tab, v_tab):
    b, n, d = h.shape
    expert, gate = peer_select(h, w_query, sub_keys)
    uv = jnp.concatenate([u_tab, v_tab], axis=-1)
    out = peer_eval(h.reshape(b * n, d), expert.reshape(-1), gate.T, uv)
    return out.reshape(b, n, d)


LN_ROWS = 512


def _res_ln_kernel(x_ref, y_ref, gate_ref, g_ref, b_ref, o_ref):
    x = x_ref[0]
    y = y_ref[0]
    t = DEEPNORM_ALPHA * x + gate_ref[0] * y
    mu = jnp.mean(t, axis=-1, keepdims=True)
    tc = t - mu
    var = jnp.mean(tc * tc, axis=-1, keepdims=True)
    o_ref[0] = tc * lax.rsqrt(var + LN_EPS) * g_ref[...] + b_ref[...]


def res_ln(x, y, gate, g, b):
    bsz, n, d = x.shape
    return pl.pallas_call(
        _res_ln_kernel,
        grid=(bsz, n // LN_ROWS),
        in_specs=[
            pl.BlockSpec((1, LN_ROWS, d), lambda i, j: (i, j, 0)),
            pl.BlockSpec((1, LN_ROWS, d), lambda i, j: (i, j, 0)),
            pl.BlockSpec((1, 1, d), lambda i, j: (i, 0, 0)),
            pl.BlockSpec((1, d), lambda i, j: (0, 0)),
            pl.BlockSpec((1, d), lambda i, j: (0, 0)),
        ],
        out_specs=pl.BlockSpec((1, LN_ROWS, d), lambda i, j: (i, j, 0)),
        out_shape=jax.ShapeDtypeStruct(x.shape, x.dtype),
        name="res_ln",
    )(x, y, gate.reshape(bsz, 1, d), g.reshape(1, d), b.reshape(1, d))


def kernel(x, c, ctx, c_ctx, w_ada, b_ada, w_in, conv_w, a_log, dt_bias, sink, dn_norm_w, w_out,
           ln1_g, ln1_b, peer_wq, peer_sub_keys, peer_u, peer_v, ln2_g, ln2_b):
    l = 0
    mod = jax.nn.silu(c) @ w_ada[l] + b_ada[l]
    mod_c = jax.nn.silu(c_ctx) @ w_ada[l] + b_ada[l]
    sh1, sc1, gt1, sh2, sc2, gt2 = [t[:, None, :] for t in jnp.split(mod, 6, axis=-1)]
    csh1, csc1, cgt1, csh2, csc2, cgt2 = jnp.split(mod_c, 6, axis=-1)
    h = modulate(x, sh1, sc1)
    hc = modulate(ctx, csh1, csc1)
    y = mixing_sublayer(h, hc, w_in[l], conv_w[l], a_log[l], dt_bias[l], sink[l],
                        dn_norm_w[l], w_out[l])
    x = res_ln(x, y, gt1[:, 0], ln1_g[l], ln1_b[l])
    h = modulate(x, sh2, sc2)
    p = peer(h, peer_wq[l], peer_sub_keys[l], peer_u[l], peer_v[l])
    x = res_ln(x, p, gt2[:, 0], ln2_g[l], ln2_b[l])
    return x
```

```python
import math
from functools import partial

import jax
import jax.numpy as jnp
from jax import lax
from jax.experimental import pallas as pl
from jax.experimental.pallas import tpu as pltpu

D_MODEL = 1024
BATCH = 4
SEQ = 8192
DEPTH = 1

GRID_W = 64
CTX_LEN = 256

ATTN_HEADS = 8
ATTN_KV_HEADS = 2
ATTN_GROUP = ATTN_HEADS // ATTN_KV_HEADS
ATTN_HEAD_DIM = 64
WINDOW = 128
ATTN_BLOCK = 128
ROPE_BASE = 10000.0
DN_HEADS = 4
DN_HEAD_DIM = 128
DN_CONV = 5
DN_CHUNK = 64
ATTN_Q_W = ATTN_HEADS * ATTN_HEAD_DIM
ATTN_KV_W = ATTN_KV_HEADS * ATTN_HEAD_DIM
DN_W = DN_HEADS * DN_HEAD_DIM
MIX_W = ATTN_Q_W + DN_W
D_IN = ATTN_Q_W + 2 * ATTN_KV_W + 4 * DN_W + 4 * DN_HEADS
IN_SPLITS = (ATTN_Q_W, ATTN_Q_W + ATTN_KV_W, ATTN_Q_W + 2 * ATTN_KV_W,
             ATTN_Q_W + 2 * ATTN_KV_W + 3 * DN_W, ATTN_Q_W + 2 * ATTN_KV_W + 4 * DN_W)
PEER_HEADS = 8
PEER_N_KEYS = 128
PEER_EXPERTS = PEER_N_KEYS ** 2
PEER_KEY_HALF = 128
PEER_TOPK = 16
PEER_BLOCK = 128
DEEPNORM_ALPHA = (2 * DEPTH) ** 0.25
DEEPNORM_BETA = (8 * DEPTH) ** -0.25
LN_EPS = 1e-6
NEG_INF = -1e30


def standardize(x):
    xf = x.astype(jnp.float32)
    xc = xf - xf.mean(-1, keepdims=True)
    var = (xc * xc).mean(-1, keepdims=True)
    return (xc * lax.rsqrt(var + LN_EPS)).astype(x.dtype)


def layer_norm(x, g, b):
    return standardize(x) * g + b


def modulate(x, shift, scale):
    return standardize(x) * (1 + scale) + shift


def axial_rope(x, row, col):
    half = x.shape[-1] // 2
    nf = half // 2
    freqs = ROPE_BASE ** (-jnp.arange(nf, dtype=jnp.float32) / nf)

    def rotate(xa, pos):
        ang = pos.astype(jnp.float32)[:, None] * freqs
        cos = jnp.cos(ang)[None, :, None, :].astype(x.dtype)
        sin = jnp.sin(ang)[None, :, None, :].astype(x.dtype)
        x1, x2 = xa[..., :nf], xa[..., nf:]
        return jnp.concatenate([x1 * cos - x2 * sin, x1 * sin + x2 * cos], -1)

    return jnp.concatenate([rotate(x[..., :half], row), rotate(x[..., half:], col)], -1)


def softmax_with_sink(logits, sink):
    snk = jnp.broadcast_to(sink.astype(jnp.float32).reshape(ATTN_KV_HEADS, ATTN_GROUP, 1, 1),
                           logits.shape[:-1] + (1,))
    return jax.nn.softmax(jnp.concatenate([logits, snk], -1), axis=-1)[..., :-1]


def banded_window_attention(q, k, v, k_ctx, v_ctx, sink):
    b, n = q.shape[:2]
    nblk = n // ATTN_BLOCK
    scale = ATTN_HEAD_DIM ** -0.5
    qb = q.reshape(b, nblk, ATTN_BLOCK, ATTN_KV_HEADS, ATTN_GROUP, ATTN_HEAD_DIM)
    pad = ((0, 0), (ATTN_BLOCK, ATTN_BLOCK), (0, 0), (0, 0))
    k_pad = jnp.pad(k, pad)
    v_pad = jnp.pad(v, pad)
    n_loc = 3 * ATTN_BLOCK
    q_off = jnp.arange(ATTN_BLOCK)
    k_off = jnp.arange(n_loc) - ATTN_BLOCK
    in_band = jnp.abs(q_off[:, None] - k_off[None, :]) <= WINDOW

    def one_block(i):
        qi = lax.dynamic_index_in_dim(qb, i, axis=1, keepdims=False)
        ki = lax.dynamic_slice_in_dim(k_pad, i * ATTN_BLOCK, n_loc, axis=1)
        vi = lax.dynamic_slice_in_dim(v_pad, i * ATTN_BLOCK, n_loc, axis=1)
        k_abs = i * ATTN_BLOCK + k_off
        mask = in_band & ((k_abs >= 0) & (k_abs < n))[None, :]
        s_loc = jnp.einsum('bqhgd,bkhd->bhgqk', qi, ki).astype(jnp.float32) * scale
        s_loc = jnp.where(mask, s_loc, NEG_INF)
        s_ctx = jnp.einsum('bqhgd,bkhd->bhgqk', qi, k_ctx).astype(jnp.float32) * scale
        p = softmax_with_sink(jnp.concatenate([s_loc, s_ctx], -1), sink).astype(v.dtype)
        return (jnp.einsum('bhgqk,bkhd->bqhgd', p[..., :n_loc], vi)
                + jnp.einsum('bhgqk,bkhd->bqhgd', p[..., n_loc:], v_ctx))

    o = lax.map(one_block, jnp.arange(nblk))
    return jnp.moveaxis(o, 0, 1).reshape(b, n, ATTN_Q_W)


def centred_conv(x, w):
    pad = (w.shape[0] - 1) // 2
    return lax.conv_general_dilated(x, w[:, None, :], window_strides=(1,), padding=[(pad, pad)],
                                    dimension_numbers=('NWC', 'WIO', 'NWC'),
                                    feature_group_count=x.shape[-1])


def l2norm(x):
    xf = x.astype(jnp.float32)
    return xf * lax.rsqrt(jnp.sum(xf * xf, -1, keepdims=True) + LN_EPS)


def gdn_inputs(qkv, gb, conv_w, a_log, dt_bias):
    b, n, _ = qkv.shape
    qkv = jax.nn.silu(centred_conv(qkv, conv_w))
    q, k, v = jnp.split(qkv, 3, axis=-1)
    q = l2norm(q.reshape(b, n, DN_HEADS, DN_HEAD_DIM)) * DN_HEAD_DIM ** -0.5
    k = l2norm(k.reshape(b, n, DN_HEADS, DN_HEAD_DIM))
    v = v.reshape(b, n, DN_HEADS, DN_HEAD_DIM)
    gb = gb.astype(jnp.float32)
    beta = jax.nn.sigmoid(gb[..., :2 * DN_HEADS]).reshape(b, n, 2, DN_HEADS)
    a = gb[..., 2 * DN_HEADS:].reshape(b, n, 2, DN_HEADS)
    g = -jnp.exp(a_log.astype(jnp.float32)) * jax.nn.softplus(a + dt_bias.astype(jnp.float32))
    return q, k, v, g, beta


def gated_delta_rule(q, k, v, g, beta, s0):
    b, n, h, dk = q.shape
    dv = v.shape[-1]
    nc = n // DN_CHUNK

    def chunks(t):
        t = t.astype(jnp.float32).reshape((b, nc, DN_CHUNK) + t.shape[2:])
        return jnp.swapaxes(t, 2, 3)

    q, k, v, g, beta = chunks(q), chunks(k), chunks(v), chunks(g), chunks(beta)
    gam = jnp.cumsum(g, axis=-1)
    causal = jnp.tril(jnp.ones((DN_CHUNK, DN_CHUNK), bool))
    strict = jnp.tril(jnp.ones((DN_CHUNK, DN_CHUNK), bool), -1)
    diff = gam[..., :, None] - gam[..., None, :]
    decay = jnp.where(causal, jnp.exp(jnp.where(causal, diff, 0.0)), 0.0)
    kb = k * beta[..., None]
    a_mat = jnp.where(strict, jnp.einsum('bnhid,bnhjd->bnhij', kb, k) * decay, 0.0)
    rhs = jnp.concatenate([v * beta[..., None], kb * jnp.exp(gam)[..., None]], -1)
    sol = lax.linalg.triangular_solve(a_mat, rhs, left_side=True, lower=True, unit_diagonal=True)
    u_c, w_c = sol[..., :dv], sol[..., dv:]
    qk = jnp.einsum('bnhid,bnhjd->bnhij', q, k) * decay
    q_dec = q * jnp.exp(gam)[..., None]
    k_dec = k * jnp.exp(gam[..., -1:] - gam)[..., None]
    g_tot = jnp.exp(gam[..., -1])

    def step(state, inp):
        qk_i, qd_i, kd_i, u_i, w_i, gt_i = inp
        v_new = u_i - jnp.einsum('bhid,bhde->bhie', w_i, state)
        o = jnp.einsum('bhid,bhde->bhie', qd_i, state) + jnp.einsum('bhij,bhje->bhie', qk_i, v_new)
        state = state * gt_i[..., None, None] + jnp.einsum('bhid,bhie->bhde', kd_i, v_new)
        return state, o

    xs = tuple(jnp.moveaxis(t, 1, 0) for t in (qk, q_dec, k_dec, u_c, w_c, g_tot))
    s_final, o = lax.scan(step, s0, xs)
    o = jnp.swapaxes(jnp.moveaxis(o, 0, 1), 2, 3).reshape(b, n, h, dv)
    return o, s_final


def bidir_delta(q, k, v, g, beta, s0_f, s0_b):
    o_f, s_f = gated_delta_rule(q, k, v, g[:, :, 0], beta[:, :, 0], s0_f)
    flip = lambda t: jnp.flip(t, axis=1)
    o_b, s_b = gated_delta_rule(flip(q), flip(k), flip(v), flip(g[:, :, 1]), flip(beta[:, :, 1]), s0_b)
    return o_f + flip(o_b), s_f, s_b


def gated_rmsnorm(o, z, w):
    b, n = z.shape[:2]
    y = o * lax.rsqrt(jnp.mean(o * o, -1, keepdims=True) + LN_EPS) * w.astype(jnp.float32)
    return y.astype(z.dtype).reshape(b, n, DN_W) * jax.nn.silu(z)


def mixing_sublayer(h, hc, w_in, conv_w, a_log, dt_bias, sink, dn_norm_w, w_out):
    b, n, _ = h.shape
    m = hc.shape[1]
    rows = n // GRID_W
    row = jnp.broadcast_to(jnp.arange(rows)[:, None], (rows, GRID_W)).reshape(n)
    col = jnp.broadcast_to(jnp.arange(GRID_W)[None, :], (rows, GRID_W)).reshape(n)
    qa, ka, va, qkv_d, z, gb = jnp.split(h @ w_in, IN_SPLITS, axis=-1)
    qa_c, ka_c, va_c, qkv_dc, z_c, gb_c = jnp.split(hc @ w_in, IN_SPLITS, axis=-1)
    qa = axial_rope(qa.reshape(b, n, ATTN_HEADS, ATTN_HEAD_DIM), row, col)
    ka = axial_rope(ka.reshape(b, n, ATTN_KV_HEADS, ATTN_HEAD_DIM), row, col)
    va = va.reshape(b, n, ATTN_KV_HEADS, ATTN_HEAD_DIM)
    ka_c = ka_c.reshape(b, m, ATTN_KV_HEADS, ATTN_HEAD_DIM)
    va_c = va_c.reshape(b, m, ATTN_KV_HEADS, ATTN_HEAD_DIM)
    attn = banded_window_attention(qa, ka, va, ka_c, va_c, sink)
    qd_c, kd_c, vd_c, g_c, beta_c = gdn_inputs(qkv_dc, gb_c, conv_w, a_log, dt_bias)
    s0 = jnp.zeros((b, DN_HEADS, DN_HEAD_DIM, DN_HEAD_DIM), jnp.float32)
    o_c, s_f, s_b = bidir_delta(qd_c, kd_c, vd_c, g_c, beta_c, s0, s0)
    qd, kd, vd, g, beta = gdn_inputs(qkv_d, gb, conv_w, a_log, dt_bias)
    o, _, _ = bidir_delta(qd, kd, vd, g, beta, s_f, s_b)
    y = jnp.concatenate([attn, gated_rmsnorm(o, z, dn_norm_w)], -1) @ w_out
    return y


def peer_select(h, w_query, sub_keys):
    b, n, d = h.shape
    tokens = h.reshape(-1, PEER_BLOCK, d)

    def one_block(xt):
        q = (xt @ w_query).reshape(PEER_BLOCK, PEER_HEADS, 2, PEER_KEY_HALF)
        s = jnp.einsum('phxd,xkd->phxk', q, sub_keys).astype(jnp.float32)
        s_top, i_top = lax.top_k(s, PEER_TOPK)
        cand = (s_top[:, :, 0, :, None] + s_top[:, :, 1, None, :]).reshape(
            PEER_BLOCK, PEER_HEADS, PEER_TOPK * PEER_TOPK)
        best, pos = lax.top_k(cand, PEER_TOPK)
        i1 = jnp.take_along_axis(i_top[:, :, 0], pos // PEER_TOPK, axis=-1)
        i2 = jnp.take_along_axis(i_top[:, :, 1], pos % PEER_TOPK, axis=-1)
        expert = i1 * PEER_N_KEYS + i2
        gate = jax.nn.softmax(best, axis=-1).astype(xt.dtype)
        return expert.reshape(PEER_BLOCK, PEER_SEL), gate.reshape(PEER_BLOCK, PEER_SEL)

    expert, gate = lax.map(one_block, tokens)
    return expert.reshape(b * n, PEER_SEL), gate.reshape(b * n, PEER_SEL)


PS_TOKENS = 512
LANES = 128


def _top16_rows(s, n_rows):
    row = lax.broadcasted_iota(jnp.int32, (n_rows, LANES), 0)
    vals, idxs = [], []
    for _ in range(PEER_TOPK):
        m = jnp.max(s, axis=0, keepdims=True)
        i = jnp.min(jnp.where(s == m, row, n_rows), axis=0, keepdims=True)
        s = jnp.where(row == i, -jnp.inf, s)
        vals.append(m)
        idxs.append(i)
    return jnp.concatenate(vals, axis=0), jnp.concatenate(idxs, axis=0)


def _peer_select_kernel(x_ref, wq_ref, sk_ref, exp_ref, gate_ref, s_scr):
    q = jnp.dot(x_ref[...].astype(jnp.bfloat16), wq_ref[...].astype(jnp.bfloat16),
                preferred_element_type=jnp.float32)
    for x in range(2):
        qx = q[:, x * PEER_KEY_HALF:(x + 1) * PEER_KEY_HALF].astype(jnp.bfloat16)
        s_scr[x] = lax.dot_general(sk_ref[x].astype(jnp.bfloat16), qx, (((1,), (1,)), ((), ())),
                                   preferred_element_type=jnp.float32)

    def chunk(c, carry):
        off = pl.multiple_of(c * LANES, LANES)
        s1, i1 = _top16_rows(s_scr[0, :, pl.ds(off, LANES)], PEER_N_KEYS)
        s2, i2 = _top16_rows(s_scr[1, :, pl.ds(off, LANES)], PEER_N_KEYS)
        cand = jnp.concatenate([s1[i:i + 1, :] + s2 for i in range(PEER_TOPK)], axis=0)
        best, pos = _top16_rows(cand, PEER_TOPK * PEER_TOPK)
        p1 = pos // PEER_TOPK
        p2 = pos % PEER_TOPK
        k1 = jnp.zeros_like(pos)
        k2 = jnp.zeros_like(pos)
        for i in range(PEER_TOPK):
            k1 = jnp.where(p1 == i, i1[i:i + 1, :], k1)
            k2 = jnp.where(p2 == i, i2[i:i + 1, :], k2)
        exp_ref[:, pl.ds(off, LANES)] = k1 * PEER_N_KEYS + k2
        e = jnp.exp(best - best[0:1, :])
        gate_ref[:, pl.ds(off, LANES)] = e / jnp.sum(e, axis=0, keepdims=True)
        return carry

    lax.fori_loop(0, PS_TOKENS // LANES, chunk, 0)


def peer_select_pallas(x, w_query, sub_keys):
    t, d = x.shape
    hw = 2 * PEER_KEY_HALF
    return pl.pallas_call(
        _peer_select_kernel,
        grid=(PEER_HEADS, t // PS_TOKENS),
        in_specs=[
            pl.BlockSpec((PS_TOKENS, d), lambda h, i: (i, 0)),
            pl.BlockSpec((d, hw), lambda h, i: (0, h)),
            pl.BlockSpec((2, PEER_N_KEYS, PEER_KEY_HALF), lambda h, i: (0, 0, 0)),
        ],
        out_specs=[
            pl.BlockSpec((PEER_TOPK, PS_TOKENS), lambda h, i: (h, i)),
            pl.BlockSpec((PEER_TOPK, PS_TOKENS), lambda h, i: (h, i)),
        ],
        out_shape=[
            jax.ShapeDtypeStruct((PEER_HEADS * PEER_TOPK, t), jnp.int32),
            jax.ShapeDtypeStruct((PEER_HEADS * PEER_TOPK, t), jnp.float32),
        ],
        scratch_shapes=[pltpu.VMEM((2, PEER_N_KEYS, PS_TOKENS), jnp.float32)],
        compiler_params=pltpu.CompilerParams(dimension_semantics=("arbitrary", "arbitrary")),
        name="peer_select",
    )(x, w_query, sub_keys)


PEER_SEL = PEER_HEADS * PEER_TOPK
PG_TOKENS = 128
PG_SLOTS = 8
_SQRT_HALF = 0.7071067811865476


def _peer_eval_kernel(idx_ref, x_ref, gt_ref, uv_hbm, o_ref, buf, sem):
    d = x_ref.shape[-1]

    def row_copy(t, e, slot):
        row = idx_ref[t * PEER_SEL + e]
        return pltpu.make_async_copy(uv_hbm.at[pl.ds(row, 1)], buf.at[slot, pl.ds(e, 1)], sem.at[slot])

    def issue(t):
        slot = t % PG_SLOTS
        for e in range(PEER_SEL):
            row_copy(t, e, slot).start()

    def wait(slot):
        pltpu.make_async_copy(uv_hbm.at[pl.ds(0, PEER_SEL)], buf.at[slot], sem.at[slot]).wait()

    for t in range(PG_SLOTS - 1):
        issue(t)

    lane = lax.broadcasted_iota(jnp.int32, (PEER_SEL, PG_TOKENS), 1)

    def step(t, carry):
        @pl.when(t + PG_SLOTS - 1 < PG_TOKENS)
        def _():
            issue(t + PG_SLOTS - 1)

        slot = t % PG_SLOTS
        wait(slot)
        x = x_ref[pl.ds(t, 1), :]
        a = jnp.sum(buf[slot, :, :d] * x, axis=-1, keepdims=True)
        act = 0.5 * a * (1.0 + lax.erf(a * _SQRT_HALF))
        g = jnp.sum(jnp.where(lane == t, gt_ref[...], 0.0), axis=-1, keepdims=True)
        o_ref[pl.ds(t, 1), :] = jnp.sum((g * act) * buf[slot, :, d:], axis=0, keepdims=True)
        return carry

    lax.fori_loop(0, PG_TOKENS, step, 0)


def peer_eval(x, expert, gate_t, uv):
    t, d = x.shape
    return pl.pallas_call(
        _peer_eval_kernel,
        grid=(t // PG_TOKENS,),
        in_specs=[
            pl.BlockSpec((PG_TOKENS * PEER_SEL,), lambda i: (i,), memory_space=pltpu.SMEM),
            pl.BlockSpec((PG_TOKENS, d), lambda i: (i, 0)),
            pl.BlockSpec((PEER_SEL, PG_TOKENS), lambda i: (0, i)),
            pl.BlockSpec(memory_space=pl.ANY),
        ],
        out_specs=pl.BlockSpec((PG_TOKENS, d), lambda i: (i, 0)),
        out_shape=jax.ShapeDtypeStruct((t, d), jnp.float32),
        scratch_shapes=[
            pltpu.VMEM((PG_SLOTS, PEER_SEL, 2 * d), jnp.float32),
            pltpu.SemaphoreType.DMA((PG_SLOTS,)),
        ],
        compiler_params=pltpu.CompilerParams(dimension_semantics=("arbitrary",)),
        name="peer_eval",
    )(expert, x, gate_t, uv)


def peer(h, w_query, sub_keys, u_tab, v_tab):
    b, n, d = h.shape
    x2 = h.reshape(b * n, d)
    expert_t, gate_t = peer_select_pallas(x2, w_query, sub_keys)
    uv = jnp.concatenate([u_tab, v_tab], axis=-1)
    out = peer_eval(x2, expert_t.T.reshape(-1), gate_t, uv)
    return out.reshape(b, n, d)


LN_ROWS = 512


def _res_ln_kernel(x_ref, y_ref, gate_ref, g_ref, b_ref, o_ref):
    x = x_ref[0]
    y = y_ref[0]
    t = DEEPNORM_ALPHA * x + gate_ref[0] * y
    mu = jnp.mean(t, axis=-1, keepdims=True)
    tc = t - mu
    var = jnp.mean(tc * tc, axis=-1, keepdims=True)
    o_ref[0] = tc * lax.rsqrt(var + LN_EPS) * g_ref[...] + b_ref[...]


def res_ln(x, y, gate, g, b):
    bsz, n, d = x.shape
    return pl.pallas_call(
        _res_ln_kernel,
        grid=(bsz, n // LN_ROWS),
        in_specs=[
            pl.BlockSpec((1, LN_ROWS, d), lambda i, j: (i, j, 0)),
            pl.BlockSpec((1, LN_ROWS, d), lambda i, j: (i, j, 0)),
            pl.BlockSpec((1, 1, d), lambda i, j: (i, 0, 0)),
            pl.BlockSpec((1, d), lambda i, j: (0, 0)),
            pl.BlockSpec((1, d), lambda i, j: (0, 0)),
        ],
        out_specs=pl.BlockSpec((1, LN_ROWS, d), lambda i, j: (i, j, 0)),
        out_shape=jax.ShapeDtypeStruct(x.shape, x.dtype),
        name="res_ln",
    )(x, y, gate.reshape(bsz, 1, d), g.reshape(1, d), b.reshape(1, d))


def kernel(x, c, ctx, c_ctx, w_ada, b_ada, w_in, conv_w, a_log, dt_bias, sink, dn_norm_w, w_out,
           ln1_g, ln1_b, peer_wq, peer_sub_keys, peer_u, peer_v, ln2_g, ln2_b):
    l = 0
    mod = jax.nn.silu(c) @ w_ada[l] + b_ada[l]
    mod_c = jax.nn.silu(c_ctx) @ w_ada[l] + b_ada[l]
    sh1, sc1, gt1, sh2, sc2, gt2 = [t[:, None, :] for t in jnp.split(mod, 6, axis=-1)]
    csh1, csc1, cgt1, csh2, csc2, cgt2 = jnp.split(mod_c, 6, axis=-1)
    h = modulate(x, sh1, sc1)
    hc = modulate(ctx, csh1, csc1)
    y = mixing_sublayer(h, hc, w_in[l], conv_w[l], a_log[l], dt_bias[l], sink[l],
                        dn_norm_w[l], w_out[l])
    x = res_ln(x, y, gt1[:, 0], ln1_g[l], ln1_b[l])
    h = modulate(x, sh2, sc2)
    p = peer(h, peer_wq[l], peer_sub_keys[l], peer_u[l], peer_v[l])
    x = res_ln(x, p, gt2[:, 0], ln2_g[l], ln2_b[l])
    return x
```

```python
import jax
import jax.numpy as jnp
from jax import lax
from jax.experimental import pallas as pl
from jax.experimental.pallas import tpu as pltpu

D_MODEL = 1024
DEPTH = 1
GRID_W = 64

ATTN_HEADS = 8
ATTN_KV_HEADS = 2
ATTN_GROUP = ATTN_HEADS // ATTN_KV_HEADS
ATTN_HEAD_DIM = 64
WINDOW = 128
ATTN_BLOCK = 128
ROPE_BASE = 10000.0
DN_HEADS = 4
DN_HEAD_DIM = 128
DN_CONV = 5
DN_CHUNK = 64
ATTN_Q_W = ATTN_HEADS * ATTN_HEAD_DIM
ATTN_KV_W = ATTN_KV_HEADS * ATTN_HEAD_DIM
DN_W = DN_HEADS * DN_HEAD_DIM
MIX_W = ATTN_Q_W + DN_W
D_IN = ATTN_Q_W + 2 * ATTN_KV_W + 4 * DN_W + 4 * DN_HEADS
PEER_HEADS = 8
PEER_N_KEYS = 128
PEER_EXPERTS = PEER_N_KEYS ** 2
PEER_KEY_HALF = 128
PEER_TOPK = 16
DEEPNORM_ALPHA = (2 * DEPTH) ** 0.25
LN_EPS = 1e-6
NEG_INF = -1e30

LANES = 128
SUBLANES = 8


def _standardize_rows(t):
    mu = jnp.mean(t, axis=-1, keepdims=True)
    tc = t - mu
    var = jnp.mean(tc * tc, axis=-1, keepdims=True)
    return tc * lax.rsqrt(var + LN_EPS)


ADA_COLS = 512


def _ada_kernel(c_ref, w_ref, b_ref, o_ref):
    c = c_ref[...]
    s = c * jax.nn.sigmoid(c)
    o_ref[...] = jnp.dot(s.astype(jnp.bfloat16), w_ref[...].astype(jnp.bfloat16),
                         preferred_element_type=jnp.float32) + b_ref[...]


def ada_mod(c_rows, w_ada, b_ada):
    r, d = c_rows.shape
    n = w_ada.shape[1]
    return pl.pallas_call(
        _ada_kernel,
        grid=(n // ADA_COLS,),
        in_specs=[pl.BlockSpec((r, d), lambda j: (0, 0)),
                  pl.BlockSpec((d, ADA_COLS), lambda j: (0, j)),
                  pl.BlockSpec((1, ADA_COLS), lambda j: (0, j))],
        out_specs=pl.BlockSpec((r, ADA_COLS), lambda j: (0, j)),
        out_shape=jax.ShapeDtypeStruct((r, n), jnp.float32),
        name="ada_mod",
    )(c_rows, w_ada, b_ada.reshape(1, n))


IP_ROWS = 256
ATTN_QKV_W = ATTN_Q_W + 2 * ATTN_KV_W
DN_QKV_W = 3 * DN_W
GB_PAD = LANES
IN_PAD_W = ATTN_QKV_W + DN_QKV_W + DN_W + GB_PAD
ROPE_W = ATTN_Q_W + ATTN_KV_W
ROPE_PAIR = ATTN_HEAD_DIM // 4


def _in_proj_kernel(x_ref, shift_ref, scale_ref, w_ref, cos_ref, sin_ref, qkv_ref, dn_ref, z_ref, gb_ref):
    h = _standardize_rows(x_ref[0]) * (1.0 + scale_ref[0]) + shift_ref[0]
    y = jnp.dot(h.astype(jnp.bfloat16), w_ref[...], preferred_element_type=jnp.float32)
    cos = cos_ref[...]
    sin = sin_ref[...]
    lane = lax.broadcasted_iota(jnp.int32, cos.shape, 1)
    first = (lane // ROPE_PAIR) % 2 == 0
    for c in range(ROPE_W // LANES):
        t = y[:, c * LANES:(c + 1) * LANES]
        partner = jnp.where(first, pltpu.roll(t, LANES - ROPE_PAIR, axis=1), pltpu.roll(t, ROPE_PAIR, axis=1))
        qkv_ref[0, :, c * LANES:(c + 1) * LANES] = t * cos + partner * sin
    qkv_ref[0, :, ROPE_W:] = y[:, ROPE_W:ATTN_QKV_W]
    dn_ref[0] = y[:, ATTN_QKV_W:ATTN_QKV_W + DN_QKV_W]
    z_ref[0] = y[:, ATTN_QKV_W + DN_QKV_W:ATTN_QKV_W + DN_QKV_W + DN_W]
    gb_ref[0] = y[:, ATTN_QKV_W + DN_QKV_W + DN_W:]


def rope_tables(n, rotate):
    if not rotate:
        return jnp.ones((n, LANES), jnp.float32), jnp.zeros((n, LANES), jnp.float32)
    freqs = ROPE_BASE ** (-jnp.arange(ROPE_PAIR, dtype=jnp.float32) / ROPE_PAIR)
    pos = jnp.arange(n)
    ang_r = (pos // GRID_W).astype(jnp.float32)[:, None] * freqs
    ang_c = (pos % GRID_W).astype(jnp.float32)[:, None] * freqs
    cos = jnp.concatenate([jnp.cos(ang_r), jnp.cos(ang_r), jnp.cos(ang_c), jnp.cos(ang_c)], -1)
    sin = jnp.concatenate([-jnp.sin(ang_r), jnp.sin(ang_r), -jnp.sin(ang_c), jnp.sin(ang_c)], -1)
    reps = LANES // ATTN_HEAD_DIM
    return jnp.tile(cos, (1, reps)), jnp.tile(sin, (1, reps))


def in_proj(x, shift, scale, w_pad, cos, sin):
    b, n, d = x.shape
    rows = min(IP_ROWS, n)
    row_spec = lambda w: pl.BlockSpec((1, rows, w), lambda i, j: (i, j, 0))
    per_batch = pl.BlockSpec((1, 1, d), lambda i, j: (i, 0, 0))
    return pl.pallas_call(
        _in_proj_kernel,
        grid=(b, n // rows),
        in_specs=[row_spec(d), per_batch, per_batch,
                  pl.BlockSpec((d, IN_PAD_W), lambda i, j: (0, 0)),
                  pl.BlockSpec((rows, LANES), lambda i, j: (j, 0)),
                  pl.BlockSpec((rows, LANES), lambda i, j: (j, 0))],
        out_specs=[row_spec(ATTN_QKV_W), row_spec(DN_QKV_W), row_spec(DN_W), row_spec(GB_PAD)],
        out_shape=[jax.ShapeDtypeStruct((b, n, w), jnp.float32)
                   for w in (ATTN_QKV_W, DN_QKV_W, DN_W, GB_PAD)],
        name="in_proj",
    )(x, shift.reshape(b, 1, d), scale.reshape(b, 1, d), w_pad, cos, sin)


def _attn_kernel(sink_ref, q_ref, kp_ref, kc_ref, kn_ref, vp_ref, vc_ref, vn_ref, kx_ref, vx_ref, o_ref):
    j = pl.program_id(1)
    nblk = pl.num_programs(1)
    blk = ATTN_BLOCK
    dh = ATTN_HEAD_DIM
    scale = ATTN_HEAD_DIM ** -0.5
    q_off = lax.broadcasted_iota(jnp.int32, (blk, 3 * blk), 0)
    k_off = lax.broadcasted_iota(jnp.int32, (blk, 3 * blk), 1) - blk
    k_abs = j * blk + k_off
    mask = (jnp.abs(q_off - k_off) <= WINDOW) & (k_abs >= 0) & (k_abs < nblk * blk)
    mask = jnp.concatenate([mask] * ATTN_GROUP, axis=0)
    k_loc = jnp.concatenate([kp_ref[0], kc_ref[0], kn_ref[0]], axis=0).astype(jnp.bfloat16)
    v_loc = jnp.concatenate([vp_ref[0], vc_ref[0], vn_ref[0]], axis=0).astype(jnp.bfloat16)
    k_ctx = kx_ref[0].astype(jnp.bfloat16)
    v_ctx = vx_ref[0].astype(jnp.bfloat16)
    q_all = q_ref[0].astype(jnp.bfloat16)
    contract_last = (((1,), (1,)), ((), ()))
    for kvh in range(ATTN_KV_HEADS):
        q = jnp.concatenate([q_all[:, (kvh * ATTN_GROUP + g) * dh:(kvh * ATTN_GROUP + g + 1) * dh]
                             for g in range(ATTN_GROUP)], axis=0)
        ks = slice(kvh * dh, (kvh + 1) * dh)
        s_loc = lax.dot_general(q, k_loc[:, ks], contract_last, preferred_element_type=jnp.float32) * scale
        s_loc = jnp.where(mask, s_loc, NEG_INF)
        s_ctx = lax.dot_general(q, k_ctx[:, ks], contract_last, preferred_element_type=jnp.float32) * scale
        snk = jnp.concatenate([jnp.full((blk, 1), sink_ref[kvh * ATTN_GROUP + g], jnp.float32)
                               for g in range(ATTN_GROUP)], axis=0)
        m = jnp.maximum(jnp.maximum(jnp.max(s_loc, axis=-1, keepdims=True),
                                    jnp.max(s_ctx, axis=-1, keepdims=True)), snk)
        p_loc = jnp.exp(s_loc - m)
        p_ctx = jnp.exp(s_ctx - m)
        denom = (jnp.sum(p_loc, axis=-1, keepdims=True) + jnp.sum(p_ctx, axis=-1, keepdims=True)
                 + jnp.exp(snk - m))
        inv = 1.0 / denom
        o = (jnp.dot((p_loc * inv).astype(jnp.bfloat16), v_loc[:, ks], preferred_element_type=jnp.float32)
             + jnp.dot((p_ctx * inv).astype(jnp.bfloat16), v_ctx[:, ks], preferred_element_type=jnp.float32))
        for g in range(ATTN_GROUP):
            hh = kvh * ATTN_GROUP + g
            o_ref[0, :, hh * dh:(hh + 1) * dh] = o[g * blk:(g + 1) * blk]


def window_attention(qkv, kv_ctx, sink):
    b, n, _ = qkv.shape
    m = kv_ctx.shape[1]
    nblk = n // ATTN_BLOCK
    k_col = ATTN_Q_W // ATTN_KV_W
    v_col = k_col + 1

    def neighbour(col, d):
        return pl.BlockSpec((1, ATTN_BLOCK, ATTN_KV_W),
                            lambda i, j: (i, jnp.clip(j + d, 0, nblk - 1), col))

    return pl.pallas_call(
        _attn_kernel,
        grid=(b, nblk),
        in_specs=[
            pl.BlockSpec(memory_space=pltpu.SMEM),
            pl.BlockSpec((1, ATTN_BLOCK, ATTN_Q_W), lambda i, j: (i, j, 0)),
            neighbour(k_col, -1), neighbour(k_col, 0), neighbour(k_col, 1),
            neighbour(v_col, -1), neighbour(v_col, 0), neighbour(v_col, 1),
            pl.BlockSpec((1, m, ATTN_KV_W), lambda i, j: (i, 0, k_col)),
            pl.BlockSpec((1, m, ATTN_KV_W), lambda i, j: (i, 0, v_col)),
        ],
        out_specs=pl.BlockSpec((1, ATTN_BLOCK, ATTN_Q_W), lambda i, j: (i, j, 0)),
        out_shape=jax.ShapeDtypeStruct((b, n, ATTN_Q_W), jnp.float32),
        name="window_attention",
    )(sink, qkv, qkv, qkv, qkv, qkv, qkv, qkv, kv_ctx, kv_ctx)


GP_ROWS = 256
HALO = SUBLANES
N_DIRS = 2
N_GATES = N_DIRS * DN_HEADS
BETA_COL = 0
G_COL = N_GATES


def _gdn_prep_kernel(prev_ref, cur_ref, next_ref, gb_ref, cw_ref, alog_ref, dt_ref,
                     q_ref, k_ref, v_ref, gate_ref):
    j = pl.program_id(1)
    rows = cur_ref.shape[1]
    prev = jnp.where(j > 0, prev_ref[0], 0.0)
    nxt = jnp.where(j < pl.num_programs(1) - 1, next_ref[0], 0.0)
    xe = jnp.concatenate([prev, cur_ref[0], nxt], axis=0)
    pad = DN_CONV // 2
    acc = xe[HALO - pad:HALO - pad + rows] * cw_ref[0:1, :]
    for t in range(1, DN_CONV):
        acc = acc + xe[HALO - pad + t:HALO - pad + t + rows] * cw_ref[t:t + 1, :]
    y = acc * jax.nn.sigmoid(acc)
    for hh in range(DN_HEADS):
        cs = slice(hh * DN_HEAD_DIM, (hh + 1) * DN_HEAD_DIM)
        q = y[:, hh * DN_HEAD_DIM:(hh + 1) * DN_HEAD_DIM]
        k = y[:, DN_W + hh * DN_HEAD_DIM:DN_W + (hh + 1) * DN_HEAD_DIM]
        q_ref[0, :, cs] = q * lax.rsqrt(jnp.sum(q * q, axis=-1, keepdims=True) + LN_EPS) * DN_HEAD_DIM ** -0.5
        k_ref[0, :, cs] = k * lax.rsqrt(jnp.sum(k * k, axis=-1, keepdims=True) + LN_EPS)
    v_ref[0] = y[:, 2 * DN_W:]
    gb = gb_ref[0]
    lane = lax.broadcasted_iota(jnp.int32, gb.shape, 1)
    a = gb + dt_ref[...]
    softplus = jnp.maximum(a, 0.0) + jnp.log(1.0 + jnp.exp(-jnp.abs(a)))
    gate_ref[0] = jnp.where(lane < N_GATES, jax.nn.sigmoid(gb), -jnp.exp(alog_ref[...]) * softplus)


def gdn_prep(qkv_d, gb, conv_w, a_log, dt_bias):
    b, n, w = qkv_d.shape
    rows = min(GP_ROWS, n)
    hb = rows // HALO
    last = n // HALO - 1
    lane_vec = lambda p: jnp.pad(p.reshape(-1), (N_GATES, GB_PAD - 2 * N_GATES)).reshape(1, GB_PAD)
    row_spec = lambda width: pl.BlockSpec((1, rows, width), lambda i, j: (i, j, 0))
    return pl.pallas_call(
        _gdn_prep_kernel,
        grid=(b, n // rows),
        in_specs=[
            pl.BlockSpec((1, HALO, w), lambda i, j: (i, jnp.maximum(j * hb - 1, 0), 0)),
            row_spec(w),
            pl.BlockSpec((1, HALO, w), lambda i, j: (i, jnp.minimum((j + 1) * hb, last), 0)),
            row_spec(GB_PAD),
            pl.BlockSpec((HALO, w), lambda i, j: (0, 0)),
            pl.BlockSpec((1, GB_PAD), lambda i, j: (0, 0)),
            pl.BlockSpec((1, GB_PAD), lambda i, j: (0, 0)),
        ],
        out_specs=[row_spec(DN_W), row_spec(DN_W), row_spec(DN_W), row_spec(GB_PAD)],
        out_shape=[jax.ShapeDtypeStruct((b, n, DN_W), jnp.float32)] * 3
                  + [jax.ShapeDtypeStruct((b, n, GB_PAD), jnp.float32)],
        name="gdn_prep",
    )(qkv_d, qkv_d, qkv_d, gb, jnp.pad(conv_w, ((0, HALO - DN_CONV), (0, 0))), lane_vec(a_log), lane_vec(dt_bias))


def _bf16_abt(a, b):
    return lax.dot_general(a.astype(jnp.bfloat16), b.astype(jnp.bfloat16), (((1,), (1,)), ((), ())),
                           preferred_element_type=jnp.float32)


def _gdn_chunk_kernel(q_ref, k_ref, v_ref, gate_ref, u_ref, w_ref, qd_ref, kd_ref, qk_ref, gt_ref,
                      a_scr, rhs_scr):
    c = DN_CHUNK
    gates = gate_ref[0]
    gates_t = gates.T
    ri = lax.broadcasted_iota(jnp.int32, (c, c), 0)
    ci = lax.broadcasted_iota(jnp.int32, (c, c), 1)
    for hh in range(DN_HEADS):
        hs = slice(hh * DN_HEAD_DIM, (hh + 1) * DN_HEAD_DIM)
        q = q_ref[0, :, hs]
        k = k_ref[0, :, hs]
        v = v_ref[0, :, hs]
        kk = _bf16_abt(k, k)
        qkm = _bf16_abt(q, k)
        for d in range(N_DIRS):
            col = d * DN_HEADS + hh
            incl = (ci <= ri) if d == 0 else (ci >= ri)
            incl_t = (ri <= ci) if d == 0 else (ri >= ci)
            strict = (ci < ri) if d == 0 else (ci > ri)
            beta_c = gates[:, BETA_COL + col:BETA_COL + col + 1]
            g_c = gates[:, G_COL + col:G_COL + col + 1]
            g_r = gates_t[G_COL + col:G_COL + col + 1, :]
            gam_c = jnp.sum(jnp.where(incl, g_r, 0.0), axis=1, keepdims=True)
            gam_r = jnp.sum(jnp.where(incl_t, g_c, 0.0), axis=0, keepdims=True)
            decay = jnp.where(incl, jnp.exp(jnp.where(incl, gam_c - gam_r, 0.0)), 0.0)
            e_c = jnp.exp(gam_c)
            g_end = jnp.sum(g_c, axis=0, keepdims=True)
            a_scr[col] = jnp.where(strict, beta_c * kk * decay, 0.0)
            rhs_scr[col] = jnp.concatenate([v * beta_c, k * beta_c * e_c], axis=1)
            qk_ref[0, d, hh] = qkm * decay
            qd_ref[0, d, :, hs] = q * e_c
            kd_ref[0, d, :, hs] = k * jnp.exp(g_end - gam_c)
            gt_ref[0, d, hh] = jnp.broadcast_to(jnp.exp(g_end), (SUBLANES, LANES))

    def substitute(j, carry):
        for col in range(N_DIRS * DN_HEADS):
            jj = j if col < DN_HEADS else c - 1 - j
            xj = rhs_scr[col, pl.ds(jj, 1), :]
            a_col = jnp.sum(jnp.where(ci == jj, a_scr[col], 0.0), axis=1, keepdims=True)
            rhs_scr[col] = rhs_scr[col] - a_col * xj
        return carry

    lax.fori_loop(0, c, substitute, 0)
    for hh in range(DN_HEADS):
        hs = slice(hh * DN_HEAD_DIM, (hh + 1) * DN_HEAD_DIM)
        for d in range(N_DIRS):
            col = d * DN_HEADS + hh
            u_ref[0, d, :, hs] = rhs_scr[col, :, :DN_HEAD_DIM]
            w_ref[0, d, :, hs] = rhs_scr[col, :, DN_HEAD_DIM:]


def gdn_chunks(q, k, v, gates):
    b, n, _ = q.shape
    c = DN_CHUNK
    nc = n // c
    row = lambda w: pl.BlockSpec((1, c, w), lambda i, j: (i, j, 0))
    drow = lambda w: pl.BlockSpec((1, N_DIRS, c, w), lambda i, j: (i, 0, j, 0))
    wide = jax.ShapeDtypeStruct((b, N_DIRS, n, DN_W), jnp.float32)
    return pl.pallas_call(
        _gdn_chunk_kernel,
        grid=(b, nc),
        in_specs=[row(DN_W), row(DN_W), row(DN_W), row(GB_PAD)],
        out_specs=[drow(DN_W), drow(DN_W), drow(DN_W), drow(DN_W),
                   pl.BlockSpec((1, N_DIRS, DN_HEADS, c, c), lambda i, j: (i, 0, 0, j, 0)),
                   pl.BlockSpec((1, N_DIRS, DN_HEADS, SUBLANES, LANES), lambda i, j: (i, 0, 0, j, 0))],
        out_shape=[wide, wide, wide, wide,
                   jax.ShapeDtypeStruct((b, N_DIRS, DN_HEADS, n, c), jnp.float32),
                   jax.ShapeDtypeStruct((b, N_DIRS, DN_HEADS, nc * SUBLANES, LANES), jnp.float32)],
        scratch_shapes=[pltpu.VMEM((N_DIRS * DN_HEADS, c, c), jnp.float32),
                        pltpu.VMEM((N_DIRS * DN_HEADS, c, 2 * DN_HEAD_DIM), jnp.float32)],
        name="gdn_chunks",
    )(q, k, v, gates)


SCAN_CHUNKS = 8


def _gdn_scan_kernel(s0_ref, uf, wf, qdf, kdf, qkf, gtf, ub, wb, qdb, kdb, qkb, gtb,
                     of_ref, ob_ref, sfin_ref, s_scr):
    c = DN_CHUNK
    step = pl.program_id(2)
    n_local = uf.shape[2] // c

    @pl.when(step == 0)
    def _():
        s_scr[...] = s0_ref[0, :, 0]

    dirs = ((uf, wf, qdf, kdf, qkf, gtf, of_ref), (ub, wb, qdb, kdb, qkb, gtb, ob_ref))
    for i in range(n_local):
        for d, (u, w, qd, kd, qk, gt, o_ref) in enumerate(dirs):
            li = i if d == 0 else n_local - 1 - i
            rows = slice(li * c, (li + 1) * c)
            s = s_scr[d]
            sb = s.astype(jnp.bfloat16)
            v_new = u[0, 0, rows, :] - jnp.dot(w[0, 0, rows, :].astype(jnp.bfloat16), sb,
                                               preferred_element_type=jnp.float32)
            vb = v_new.astype(jnp.bfloat16)
            o_ref[0, rows, :] = (jnp.dot(qd[0, 0, rows, :].astype(jnp.bfloat16), sb,
                                         preferred_element_type=jnp.float32)
                                 + jnp.dot(qk[0, 0, 0, rows, :].astype(jnp.bfloat16), vb,
                                           preferred_element_type=jnp.float32))
            s_scr[d] = (s * gt[0, 0, 0, li * SUBLANES:li * SUBLANES + 1, :]
                        + lax.dot_general(kd[0, 0, rows, :].astype(jnp.bfloat16), vb, (((0,), (0,)), ((), ())),
                                          preferred_element_type=jnp.float32))

    @pl.when(step == pl.num_programs(2) - 1)
    def _():
        sfin_ref[0, :, 0] = s_scr[...]


def gdn_scan(s0, u, w, qd, kd, qk, gt):
    b, _, n, _ = u.shape
    c = DN_CHUNK
    per = min(SCAN_CHUNKS, n // c)
    nb = n // (per * c)
    rows = per * c

    def specs(d):
        blk = (lambda j: j) if d == 0 else (lambda j: nb - 1 - j)
        wide = pl.BlockSpec((1, 1, rows, DN_HEAD_DIM), lambda i, h, j: (i, d, blk(j), h))
        return [wide, wide, wide, wide,
                pl.BlockSpec((1, 1, 1, rows, c), lambda i, h, j: (i, d, h, blk(j), 0)),
                pl.BlockSpec((1, 1, 1, per * SUBLANES, LANES), lambda i, h, j: (i, d, h, blk(j), 0))]

    state_spec = pl.BlockSpec((1, N_DIRS, 1, DN_HEAD_DIM, DN_HEAD_DIM), lambda i, h, j: (i, 0, h, 0, 0))
    return pl.pallas_call(
        _gdn_scan_kernel,
        grid=(b, DN_HEADS, nb),
        in_specs=[state_spec] + specs(0) + specs(1),
        out_specs=[pl.BlockSpec((1, rows, DN_HEAD_DIM), lambda i, h, j: (i, j, h)),
                   pl.BlockSpec((1, rows, DN_HEAD_DIM), lambda i, h, j: (i, nb - 1 - j, h)),
                   state_spec],
        out_shape=[jax.ShapeDtypeStruct((b, n, DN_W), jnp.float32),
                   jax.ShapeDtypeStruct((b, n, DN_W), jnp.float32),
                   jax.ShapeDtypeStruct(s0.shape, jnp.float32)],
        scratch_shapes=[pltpu.VMEM((N_DIRS, DN_HEAD_DIM, DN_HEAD_DIM), jnp.float32)],
        name="gdn_scan",
    )(s0, u, w, qd, kd, qk, gt, u, w, qd, kd, qk, gt)


OP_ROWS = 256


def _out_proj_kernel(x_ref, attn_ref, of_ref, ob_ref, z_ref, nw_ref, w_ref, gate_ref, g_ref, b_ref,
                     sh_ref, sc_ref, x1_ref, h2_ref):
    parts = [attn_ref[0]]
    for hh in range(DN_HEADS):
        cs = slice(hh * DN_HEAD_DIM, (hh + 1) * DN_HEAD_DIM)
        o = of_ref[0, :, cs] + ob_ref[0, :, cs]
        y = o * lax.rsqrt(jnp.mean(o * o, axis=-1, keepdims=True) + LN_EPS) * nw_ref[...]
        z = z_ref[0, :, cs]
        parts.append(y * (z * jax.nn.sigmoid(z)))
    mix = jnp.concatenate(parts, axis=-1).astype(jnp.bfloat16)
    y = jnp.dot(mix, w_ref[...], preferred_element_type=jnp.float32)
    x1 = _standardize_rows(DEEPNORM_ALPHA * x_ref[0] + gate_ref[0] * y) * g_ref[...] + b_ref[...]
    x1_ref[0] = x1
    h2_ref[0] = _standardize_rows(x1) * (1.0 + sc_ref[0]) + sh_ref[0]


def out_proj(x, attn, o_f, o_b, z, dn_norm_w, w_out, gate, ln_g, ln_b, shift2, scale2):
    b, n, d = x.shape
    rows = min(OP_ROWS, n)
    row_spec = lambda w: pl.BlockSpec((1, rows, w), lambda i, j: (i, j, 0))
    per_batch = pl.BlockSpec((1, 1, d), lambda i, j: (i, 0, 0))
    const = lambda w: pl.BlockSpec((1, w), lambda i, j: (0, 0))
    return pl.pallas_call(
        _out_proj_kernel,
        grid=(b, n // rows),
        in_specs=[row_spec(d), row_spec(ATTN_Q_W), row_spec(DN_W), row_spec(DN_W), row_spec(DN_W),
                  const(DN_HEAD_DIM), pl.BlockSpec((MIX_W, d), lambda i, j: (0, 0)),
                  per_batch, const(d), const(d), per_batch, per_batch],
        out_specs=[row_spec(d), row_spec(d)],
        out_shape=[jax.ShapeDtypeStruct((b, n, d), jnp.float32)] * 2,
        name="out_proj",
    )(x, attn, o_f, o_b, z, dn_norm_w.reshape(1, DN_HEAD_DIM), w_out.astype(jnp.bfloat16),
      gate.reshape(b, 1, d), ln_g.reshape(1, d), ln_b.reshape(1, d), shift2.reshape(b, 1, d),
      scale2.reshape(b, 1, d))


PS_TOKENS = 512


def _top16_rows(s, n_rows):
    row = lax.broadcasted_iota(jnp.int32, (n_rows, LANES), 0)
    vals, idxs = [], []
    for _ in range(PEER_TOPK):
        m = jnp.max(s, axis=0, keepdims=True)
        i = jnp.min(jnp.where(s == m, row, n_rows), axis=0, keepdims=True)
        s = jnp.where(row == i, -jnp.inf, s)
        vals.append(m)
        idxs.append(i)
    return jnp.concatenate(vals, axis=0), jnp.concatenate(idxs, axis=0)


def _peer_select_kernel(x_ref, wq_ref, sk_ref, exp_ref, gate_ref, s_scr):
    q = jnp.dot(x_ref[...].astype(jnp.bfloat16), wq_ref[...].astype(jnp.bfloat16),
                preferred_element_type=jnp.float32)
    for x in range(2):
        qx = q[:, x * PEER_KEY_HALF:(x + 1) * PEER_KEY_HALF].astype(jnp.bfloat16)
        s_scr[x] = lax.dot_general(sk_ref[x].astype(jnp.bfloat16), qx, (((1,), (1,)), ((), ())),
                                   preferred_element_type=jnp.float32)

    def chunk(c, carry):
        off = pl.multiple_of(c * LANES, LANES)
        s1, i1 = _top16_rows(s_scr[0, :, pl.ds(off, LANES)], PEER_N_KEYS)
        s2, i2 = _top16_rows(s_scr[1, :, pl.ds(off, LANES)], PEER_N_KEYS)
        cand = jnp.concatenate([s1[i:i + 1, :] + s2 for i in range(PEER_TOPK)], axis=0)
        best, pos = _top16_rows(cand, PEER_TOPK * PEER_TOPK)
        p1 = pos // PEER_TOPK
        p2 = pos % PEER_TOPK
        k1 = jnp.zeros_like(pos)
        k2 = jnp.zeros_like(pos)
        for i in range(PEER_TOPK):
            k1 = jnp.where(p1 == i, i1[i:i + 1, :], k1)
            k2 = jnp.where(p2 == i, i2[i:i + 1, :], k2)
        exp_ref[:, pl.ds(off, LANES)] = k1 * PEER_N_KEYS + k2
        e = jnp.exp(best - best[0:1, :])
        gate_ref[:, pl.ds(off, LANES)] = e / jnp.sum(e, axis=0, keepdims=True)
        return carry

    lax.fori_loop(0, PS_TOKENS // LANES, chunk, 0)


def peer_select(x, w_query, sub_keys):
    t, d = x.shape
    hw = 2 * PEER_KEY_HALF
    return pl.pallas_call(
        _peer_select_kernel,
        grid=(PEER_HEADS, t // PS_TOKENS),
        in_specs=[
            pl.BlockSpec((PS_TOKENS, d), lambda h, i: (i, 0)),
            pl.BlockSpec((d, hw), lambda h, i: (0, h)),
            pl.BlockSpec((2, PEER_N_KEYS, PEER_KEY_HALF), lambda h, i: (0, 0, 0)),
        ],
        out_specs=[
            pl.BlockSpec((PEER_TOPK, PS_TOKENS), lambda h, i: (h, i)),
            pl.BlockSpec((PEER_TOPK, PS_TOKENS), lambda h, i: (h, i)),
        ],
        out_shape=[
            jax.ShapeDtypeStruct((PEER_HEADS * PEER_TOPK, t), jnp.int32),
            jax.ShapeDtypeStruct((PEER_HEADS * PEER_TOPK, t), jnp.float32),
        ],
        scratch_shapes=[pltpu.VMEM((2, PEER_N_KEYS, PS_TOKENS), jnp.float32)],
        name="peer_select",
    )(x, w_query, sub_keys)


PEER_SEL = PEER_HEADS * PEER_TOPK
PG_TOKENS = 128
PG_SLOTS = 8
PG_AHEAD = PG_SLOTS - 1
_SQRT_HALF = 0.7071067811865476


def _peer_eval_kernel(idx_ref, x_ref, gt_ref, uv_hbm, o_ref, buf, sem):
    n_chunks = x_ref.shape[-1] // LANES

    def issue(t, slot):
        for e in range(PEER_SEL):
            row = idx_ref[t * PEER_SEL + e]
            pltpu.make_async_copy(uv_hbm.at[row], buf.at[slot, :, e, :], sem.at[slot]).start()

    def wait(slot):
        pltpu.make_async_copy(buf.at[slot], buf.at[slot], sem.at[slot]).wait()

    @pl.when(pl.program_id(0) == 0)
    def _():
        for s in range(PG_AHEAD):
            issue(s, s)

    lane = lax.broadcasted_iota(jnp.int32, (PEER_SEL, PG_TOKENS), 1)

    def group(g, carry):
        for s in range(PG_SLOTS):
            t = g * PG_SLOTS + s
            issue(t + PG_AHEAD, (s + PG_AHEAD) % PG_SLOTS)
            wait(s)
            x = x_ref[pl.ds(t, 1), :]
            p = buf[s, 0] * x[:, :LANES]
            for c in range(1, n_chunks):
                p = p + buf[s, c] * x[:, c * LANES:(c + 1) * LANES]
            a = jnp.sum(p, axis=-1, keepdims=True)
            act = 0.5 * a * (1.0 + lax.erf(a * _SQRT_HALF))
            gate = jnp.sum(jnp.where(lane == t, gt_ref[...], 0.0), axis=-1, keepdims=True)
            w = gate * act
            o_ref[pl.ds(t, 1), :] = jnp.concatenate(
                [jnp.sum(w * buf[s, n_chunks + c], axis=0, keepdims=True) for c in range(n_chunks)], axis=-1)
        return carry

    lax.fori_loop(0, PG_TOKENS // PG_SLOTS, group, 0)

    @pl.when(pl.program_id(0) == pl.num_programs(0) - 1)
    def _():
        for s in range(PG_AHEAD):
            wait(s)


def peer_eval(x, expert, gate_t, uv):
    t, d = x.shape
    blocks = expert.reshape(t // PG_TOKENS, PG_TOKENS, PEER_SEL)
    ahead = jnp.roll(blocks[:, :PG_SLOTS], -1, axis=0)
    expert = jnp.concatenate([blocks, ahead], axis=1).reshape(-1)
    return pl.pallas_call(
        _peer_eval_kernel,
        grid=(t // PG_TOKENS,),
        in_specs=[
            pl.BlockSpec(((PG_TOKENS + PG_SLOTS) * PEER_SEL,), lambda i: (i,), memory_space=pltpu.SMEM),
            pl.BlockSpec((PG_TOKENS, d), lambda i: (i, 0)),
            pl.BlockSpec((PEER_SEL, PG_TOKENS), lambda i: (0, i)),
            pl.BlockSpec(memory_space=pl.ANY),
        ],
        out_specs=pl.BlockSpec((PG_TOKENS, d), lambda i: (i, 0)),
        out_shape=jax.ShapeDtypeStruct((t, d), jnp.float32),
        scratch_shapes=[
            pltpu.VMEM((PG_SLOTS, 2 * d // LANES, PEER_SEL, LANES), jnp.float32),
            pltpu.SemaphoreType.DMA((PG_SLOTS,)),
        ],
        compiler_params=pltpu.CompilerParams(dimension_semantics=("arbitrary",)),
        name="peer_eval",
    )(expert, x, gate_t, uv)


def peer(h, w_query, sub_keys, u_tab, v_tab):
    b, n, d = h.shape
    x2 = h.reshape(b * n, d)
    expert_t, gate_t = peer_select(x2, w_query, sub_keys)
    uv = jnp.concatenate([u_tab, v_tab], axis=-1).reshape(PEER_EXPERTS, 2 * d // LANES, LANES)
    out = peer_eval(x2, expert_t.T, gate_t, uv)
    return out.reshape(b, n, d)


LN_ROWS = 512


def _res_ln_kernel(x_ref, y_ref, gate_ref, g_ref, b_ref, o_ref):
    t = DEEPNORM_ALPHA * x_ref[0] + gate_ref[0] * y_ref[0]
    o_ref[0] = _standardize_rows(t) * g_ref[...] + b_ref[...]


def res_ln(x, y, gate, g, b):
    bsz, n, d = x.shape
    return pl.pallas_call(
        _res_ln_kernel,
        grid=(bsz, n // LN_ROWS),
        in_specs=[
            pl.BlockSpec((1, LN_ROWS, d), lambda i, j: (i, j, 0)),
            pl.BlockSpec((1, LN_ROWS, d), lambda i, j: (i, j, 0)),
            pl.BlockSpec((1, 1, d), lambda i, j: (i, 0, 0)),
            pl.BlockSpec((1, d), lambda i, j: (0, 0)),
            pl.BlockSpec((1, d), lambda i, j: (0, 0)),
        ],
        out_specs=pl.BlockSpec((1, LN_ROWS, d), lambda i, j: (i, j, 0)),
        out_shape=jax.ShapeDtypeStruct(x.shape, x.dtype),
        name="res_ln",
    )(x, y, gate.reshape(bsz, 1, d), g.reshape(1, d), b.reshape(1, d))


def delta_net(qkv_d, gb, conv_w, a_log, dt_bias, s0):
    q, k, v, gates = gdn_prep(qkv_d, gb, conv_w, a_log, dt_bias)
    return gdn_scan(s0, *gdn_chunks(q, k, v, gates))


def kernel(x, c, ctx, c_ctx, w_ada, b_ada, w_in, conv_w, a_log, dt_bias, sink, dn_norm_w, w_out,
           ln1_g, ln1_b, peer_wq, peer_sub_keys, peer_u, peer_v, ln2_g, ln2_b):
    assert w_ada.shape[0] == DEPTH == 1
    l = 0
    b, n, d = x.shape
    m = ctx.shape[1]
    cond = jnp.concatenate([c, c_ctx[None], jnp.zeros((SUBLANES - b - 1, d), c.dtype)], axis=0)
    mod = ada_mod(cond, w_ada[l], b_ada[l])
    sh1, sc1, gt1, sh2, sc2, gt2 = jnp.split(mod[:b], 6, axis=-1)
    csh1, csc1 = (jnp.broadcast_to(t, (b, d)) for t in jnp.split(mod[b], 6, axis=-1)[:2])
    w_pad = jnp.pad(w_in[l], ((0, 0), (0, IN_PAD_W - D_IN))).astype(jnp.bfloat16)
    qkv, dn, z, gb = in_proj(x, sh1, sc1, w_pad, *rope_tables(n, True))
    qkv_c, dn_c, _, gb_c = in_proj(ctx, csh1, csc1, w_pad, *rope_tables(m, False))
    attn = window_attention(qkv, qkv_c, sink[l])
    s0 = jnp.zeros((b, N_DIRS, DN_HEADS, DN_HEAD_DIM, DN_HEAD_DIM), jnp.float32)
    _, _, s_ctx = delta_net(dn_c, gb_c, conv_w[l], a_log[l], dt_bias[l], s0)
    o_f, o_b, _ = delta_net(dn, gb, conv_w[l], a_log[l], dt_bias[l], s_ctx)
    x1, h2 = out_proj(x, attn, o_f, o_b, z, dn_norm_w[l], w_out[l], gt1, ln1_g[l], ln1_b[l], sh2, sc2)
    p = peer(h2, peer_wq[l], peer_sub_keys[l], peer_u[l], peer_v[l])
    return res_ln(x1, p, gt2, ln2_g[l], ln2_b[l])
```

```python
import jax
import jax.numpy as jnp
from jax import lax
from jax.experimental import pallas as pl
from jax.experimental.pallas import tpu as pltpu

D_MODEL = 1024
DEPTH = 1
GRID_W = 64

ATTN_HEADS = 8
ATTN_KV_HEADS = 2
ATTN_GROUP = ATTN_HEADS // ATTN_KV_HEADS
ATTN_HEAD_DIM = 64
WINDOW = 128
ATTN_BLOCK = 128
ROPE_BASE = 10000.0
DN_HEADS = 4
DN_HEAD_DIM = 128
DN_CONV = 5
DN_CHUNK = 64
ATTN_Q_W = ATTN_HEADS * ATTN_HEAD_DIM
ATTN_KV_W = ATTN_KV_HEADS * ATTN_HEAD_DIM
DN_W = DN_HEADS * DN_HEAD_DIM
MIX_W = ATTN_Q_W + DN_W
D_IN = ATTN_Q_W + 2 * ATTN_KV_W + 4 * DN_W + 4 * DN_HEADS
PEER_HEADS = 8
PEER_N_KEYS = 128
PEER_EXPERTS = PEER_N_KEYS ** 2
PEER_KEY_HALF = 128
PEER_TOPK = 16
DEEPNORM_ALPHA = (2 * DEPTH) ** 0.25
LN_EPS = 1e-6
NEG_INF = -1e30

LANES = 128
SUBLANES = 8


def _standardize_rows(t):
    mu = jnp.mean(t, axis=-1, keepdims=True)
    tc = t - mu
    var = jnp.mean(tc * tc, axis=-1, keepdims=True)
    return tc * lax.rsqrt(var + LN_EPS)


ADA_COLS = 512


def _ada_kernel(c_ref, w_ref, b_ref, o_ref):
    c = c_ref[...]
    s = c * jax.nn.sigmoid(c)
    o_ref[...] = jnp.dot(s.astype(jnp.bfloat16), w_ref[...].astype(jnp.bfloat16),
                         preferred_element_type=jnp.float32) + b_ref[...]


def ada_mod(c_rows, w_ada, b_ada):
    r, d = c_rows.shape
    n = w_ada.shape[1]
    return pl.pallas_call(
        _ada_kernel,
        grid=(n // ADA_COLS,),
        in_specs=[pl.BlockSpec((r, d), lambda j: (0, 0)),
                  pl.BlockSpec((d, ADA_COLS), lambda j: (0, j)),
                  pl.BlockSpec((1, ADA_COLS), lambda j: (0, j))],
        out_specs=pl.BlockSpec((r, ADA_COLS), lambda j: (0, j)),
        out_shape=jax.ShapeDtypeStruct((r, n), jnp.float32),
        name="ada_mod",
    )(c_rows, w_ada, b_ada.reshape(1, n))


IP_ROWS = 256
ATTN_QKV_W = ATTN_Q_W + 2 * ATTN_KV_W
DN_QKV_W = 3 * DN_W
GB_PAD = LANES
IN_PAD_W = ATTN_QKV_W + DN_QKV_W + DN_W + GB_PAD
ROPE_W = ATTN_Q_W + ATTN_KV_W
ROPE_PAIR = ATTN_HEAD_DIM // 4


def _in_proj_kernel(x_ref, shift_ref, scale_ref, w_ref, cos_ref, sin_ref, qkv_ref, dn_ref, z_ref, gb_ref):
    h = _standardize_rows(x_ref[0]) * (1.0 + scale_ref[0]) + shift_ref[0]
    y = jnp.dot(h.astype(jnp.bfloat16), w_ref[...], preferred_element_type=jnp.float32)
    cos = cos_ref[...]
    sin = sin_ref[...]
    lane = lax.broadcasted_iota(jnp.int32, cos.shape, 1)
    first = (lane // ROPE_PAIR) % 2 == 0
    for c in range(ROPE_W // LANES):
        t = y[:, c * LANES:(c + 1) * LANES]
        partner = jnp.where(first, pltpu.roll(t, LANES - ROPE_PAIR, axis=1), pltpu.roll(t, ROPE_PAIR, axis=1))
        qkv_ref[0, :, c * LANES:(c + 1) * LANES] = t * cos + partner * sin
    qkv_ref[0, :, ROPE_W:] = y[:, ROPE_W:ATTN_QKV_W]
    dn_ref[0] = y[:, ATTN_QKV_W:ATTN_QKV_W + DN_QKV_W]
    z_ref[0] = y[:, ATTN_QKV_W + DN_QKV_W:ATTN_QKV_W + DN_QKV_W + DN_W]
    gb_ref[0] = y[:, ATTN_QKV_W + DN_QKV_W + DN_W:]


def rope_tables(n, rotate):
    if not rotate:
        return jnp.ones((n, LANES), jnp.float32), jnp.zeros((n, LANES), jnp.float32)
    freqs = ROPE_BASE ** (-jnp.arange(ROPE_PAIR, dtype=jnp.float32) / ROPE_PAIR)
    pos = jnp.arange(n)
    ang_r = (pos // GRID_W).astype(jnp.float32)[:, None] * freqs
    ang_c = (pos % GRID_W).astype(jnp.float32)[:, None] * freqs
    cos = jnp.concatenate([jnp.cos(ang_r), jnp.cos(ang_r), jnp.cos(ang_c), jnp.cos(ang_c)], -1)
    sin = jnp.concatenate([-jnp.sin(ang_r), jnp.sin(ang_r), -jnp.sin(ang_c), jnp.sin(ang_c)], -1)
    reps = LANES // ATTN_HEAD_DIM
    return jnp.tile(cos, (1, reps)), jnp.tile(sin, (1, reps))


def in_proj(x, shift, scale, w_pad, cos, sin):
    b, n, d = x.shape
    rows = min(IP_ROWS, n)
    row_spec = lambda w: pl.BlockSpec((1, rows, w), lambda i, j: (i, j, 0))
    per_batch = pl.BlockSpec((1, 1, d), lambda i, j: (i, 0, 0))
    return pl.pallas_call(
        _in_proj_kernel,
        grid=(b, n // rows),
        in_specs=[row_spec(d), per_batch, per_batch,
                  pl.BlockSpec((d, IN_PAD_W), lambda i, j: (0, 0)),
                  pl.BlockSpec((rows, LANES), lambda i, j: (j, 0)),
                  pl.BlockSpec((rows, LANES), lambda i, j: (j, 0))],
        out_specs=[row_spec(ATTN_QKV_W), row_spec(DN_QKV_W), row_spec(DN_W), row_spec(GB_PAD)],
        out_shape=[jax.ShapeDtypeStruct((b, n, w), jnp.float32)
                   for w in (ATTN_QKV_W, DN_QKV_W, DN_W, GB_PAD)],
        name="in_proj",
    )(x, shift.reshape(b, 1, d), scale.reshape(b, 1, d), w_pad, cos, sin)


def _attn_kernel(sink_ref, q_ref, kp_ref, kc_ref, kn_ref, vp_ref, vc_ref, vn_ref, kx_ref, vx_ref, o_ref):
    j = pl.program_id(1)
    nblk = pl.num_programs(1)
    blk = ATTN_BLOCK
    dh = ATTN_HEAD_DIM
    scale = ATTN_HEAD_DIM ** -0.5
    q_off = lax.broadcasted_iota(jnp.int32, (blk, 3 * blk), 0)
    k_off = lax.broadcasted_iota(jnp.int32, (blk, 3 * blk), 1) - blk
    k_abs = j * blk + k_off
    mask = (jnp.abs(q_off - k_off) <= WINDOW) & (k_abs >= 0) & (k_abs < nblk * blk)
    mask = jnp.concatenate([mask] * ATTN_GROUP, axis=0)
    k_loc = jnp.concatenate([kp_ref[0], kc_ref[0], kn_ref[0]], axis=0).astype(jnp.bfloat16)
    v_loc = jnp.concatenate([vp_ref[0], vc_ref[0], vn_ref[0]], axis=0).astype(jnp.bfloat16)
    k_ctx = kx_ref[0].astype(jnp.bfloat16)
    v_ctx = vx_ref[0].astype(jnp.bfloat16)
    q_all = q_ref[0].astype(jnp.bfloat16)
    contract_last = (((1,), (1,)), ((), ()))
    for kvh in range(ATTN_KV_HEADS):
        q = jnp.concatenate([q_all[:, (kvh * ATTN_GROUP + g) * dh:(kvh * ATTN_GROUP + g + 1) * dh]
                             for g in range(ATTN_GROUP)], axis=0)
        ks = slice(kvh * dh, (kvh + 1) * dh)
        s_loc = lax.dot_general(q, k_loc[:, ks], contract_last, preferred_element_type=jnp.float32) * scale
        s_loc = jnp.where(mask, s_loc, NEG_INF)
        s_ctx = lax.dot_general(q, k_ctx[:, ks], contract_last, preferred_element_type=jnp.float32) * scale
        snk = jnp.concatenate([jnp.full((blk, 1), sink_ref[kvh * ATTN_GROUP + g], jnp.float32)
                               for g in range(ATTN_GROUP)], axis=0)
        m = jnp.maximum(jnp.maximum(jnp.max(s_loc, axis=-1, keepdims=True),
                                    jnp.max(s_ctx, axis=-1, keepdims=True)), snk)
        p_loc = jnp.exp(s_loc - m)
        p_ctx = jnp.exp(s_ctx - m)
        denom = (jnp.sum(p_loc, axis=-1, keepdims=True) + jnp.sum(p_ctx, axis=-1, keepdims=True)
                 + jnp.exp(snk - m))
        inv = 1.0 / denom
        o = (jnp.dot((p_loc * inv).astype(jnp.bfloat16), v_loc[:, ks], preferred_element_type=jnp.float32)
             + jnp.dot((p_ctx * inv).astype(jnp.bfloat16), v_ctx[:, ks], preferred_element_type=jnp.float32))
        for g in range(ATTN_GROUP):
            hh = kvh * ATTN_GROUP + g
            o_ref[0, :, hh * dh:(hh + 1) * dh] = o[g * blk:(g + 1) * blk]


def window_attention(qkv, kv_ctx, sink):
    b, n, _ = qkv.shape
    m = kv_ctx.shape[1]
    nblk = n // ATTN_BLOCK
    k_col = ATTN_Q_W // ATTN_KV_W
    v_col = k_col + 1

    def neighbour(col, d):
        return pl.BlockSpec((1, ATTN_BLOCK, ATTN_KV_W),
                            lambda i, j: (i, jnp.clip(j + d, 0, nblk - 1), col))

    return pl.pallas_call(
        _attn_kernel,
        grid=(b, nblk),
        in_specs=[
            pl.BlockSpec(memory_space=pltpu.SMEM),
            pl.BlockSpec((1, ATTN_BLOCK, ATTN_Q_W), lambda i, j: (i, j, 0)),
            neighbour(k_col, -1), neighbour(k_col, 0), neighbour(k_col, 1),
            neighbour(v_col, -1), neighbour(v_col, 0), neighbour(v_col, 1),
            pl.BlockSpec((1, m, ATTN_KV_W), lambda i, j: (i, 0, k_col)),
            pl.BlockSpec((1, m, ATTN_KV_W), lambda i, j: (i, 0, v_col)),
        ],
        out_specs=pl.BlockSpec((1, ATTN_BLOCK, ATTN_Q_W), lambda i, j: (i, j, 0)),
        out_shape=jax.ShapeDtypeStruct((b, n, ATTN_Q_W), jnp.float32),
        name="window_attention",
    )(sink, qkv, qkv, qkv, qkv, qkv, qkv, qkv, kv_ctx, kv_ctx)


GP_ROWS = 256
HALO = SUBLANES
N_DIRS = 2
N_GATES = N_DIRS * DN_HEADS
BETA_COL = 0
G_COL = N_GATES


def _gdn_prep_kernel(prev_ref, cur_ref, next_ref, gb_ref, cw_ref, alog_ref, dt_ref,
                     q_ref, k_ref, v_ref, gate_ref):
    j = pl.program_id(1)
    rows = cur_ref.shape[1]
    prev = jnp.where(j > 0, prev_ref[0], 0.0)
    nxt = jnp.where(j < pl.num_programs(1) - 1, next_ref[0], 0.0)
    xe = jnp.concatenate([prev, cur_ref[0], nxt], axis=0)
    pad = DN_CONV // 2
    acc = xe[HALO - pad:HALO - pad + rows] * cw_ref[0:1, :]
    for t in range(1, DN_CONV):
        acc = acc + xe[HALO - pad + t:HALO - pad + t + rows] * cw_ref[t:t + 1, :]
    y = acc * jax.nn.sigmoid(acc)
    for hh in range(DN_HEADS):
        cs = slice(hh * DN_HEAD_DIM, (hh + 1) * DN_HEAD_DIM)
        q = y[:, hh * DN_HEAD_DIM:(hh + 1) * DN_HEAD_DIM]
        k = y[:, DN_W + hh * DN_HEAD_DIM:DN_W + (hh + 1) * DN_HEAD_DIM]
        q_ref[0, :, cs] = q * lax.rsqrt(jnp.sum(q * q, axis=-1, keepdims=True) + LN_EPS) * DN_HEAD_DIM ** -0.5
        k_ref[0, :, cs] = k * lax.rsqrt(jnp.sum(k * k, axis=-1, keepdims=True) + LN_EPS)
    v_ref[0] = y[:, 2 * DN_W:]
    gb = gb_ref[0]
    lane = lax.broadcasted_iota(jnp.int32, gb.shape, 1)
    a = gb + dt_ref[...]
    softplus = jnp.maximum(a, 0.0) + jnp.log(1.0 + jnp.exp(-jnp.abs(a)))
    gate_ref[0] = jnp.where(lane < N_GATES, jax.nn.sigmoid(gb), -jnp.exp(alog_ref[...]) * softplus)


def gdn_prep(qkv_d, gb, conv_w, a_log, dt_bias):
    b, n, w = qkv_d.shape
    rows = min(GP_ROWS, n)
    hb = rows // HALO
    last = n // HALO - 1
    lane_vec = lambda p: jnp.pad(p.reshape(-1), (N_GATES, GB_PAD - 2 * N_GATES)).reshape(1, GB_PAD)
    row_spec = lambda width: pl.BlockSpec((1, rows, width), lambda i, j: (i, j, 0))
    return pl.pallas_call(
        _gdn_prep_kernel,
        grid=(b, n // rows),
        in_specs=[
            pl.BlockSpec((1, HALO, w), lambda i, j: (i, jnp.maximum(j * hb - 1, 0), 0)),
            row_spec(w),
            pl.BlockSpec((1, HALO, w), lambda i, j: (i, jnp.minimum((j + 1) * hb, last), 0)),
            row_spec(GB_PAD),
            pl.BlockSpec((HALO, w), lambda i, j: (0, 0)),
            pl.BlockSpec((1, GB_PAD), lambda i, j: (0, 0)),
            pl.BlockSpec((1, GB_PAD), lambda i, j: (0, 0)),
        ],
        out_specs=[row_spec(DN_W), row_spec(DN_W), row_spec(DN_W), row_spec(GB_PAD)],
        out_shape=[jax.ShapeDtypeStruct((b, n, DN_W), jnp.float32)] * 3
                  + [jax.ShapeDtypeStruct((b, n, GB_PAD), jnp.float32)],
        name="gdn_prep",
    )(qkv_d, qkv_d, qkv_d, gb, jnp.pad(conv_w, ((0, HALO - DN_CONV), (0, 0))), lane_vec(a_log), lane_vec(dt_bias))


GC_CHUNKS = 2


def _bf16_abt(a, b):
    return lax.dot_general(a.astype(jnp.bfloat16), b.astype(jnp.bfloat16), (((1,), (1,)), ((), ())),
                           preferred_element_type=jnp.float32)


def _gdn_chunk_kernel(q_ref, k_ref, v_ref, gate_ref, u_ref, w_ref, qd_ref, kd_ref, qk_ref, gt_ref,
                      *scratch):
    c = DN_CHUNK
    n_solves = N_DIRS * DN_HEADS
    a_scr = scratch[:GC_CHUNKS * n_solves]
    rhs_scr = scratch[GC_CHUNKS * n_solves:]
    ri = lax.broadcasted_iota(jnp.int32, (c, c), 0)
    ci = lax.broadcasted_iota(jnp.int32, (c, c), 1)
    for ch in range(GC_CHUNKS):
        rows = slice(ch * c, (ch + 1) * c)
        gates = gate_ref[0, rows, :]
        gates_t = gates.T
        for hh in range(DN_HEADS):
            hs = slice(hh * DN_HEAD_DIM, (hh + 1) * DN_HEAD_DIM)
            q = q_ref[0, rows, hs]
            k = k_ref[0, rows, hs]
            v = v_ref[0, rows, hs]
            kk = _bf16_abt(k, k)
            qkm = _bf16_abt(q, k)
            for d in range(N_DIRS):
                col = d * DN_HEADS + hh
                incl = (ci <= ri) if d == 0 else (ci >= ri)
                incl_t = (ri <= ci) if d == 0 else (ri >= ci)
                strict = (ci < ri) if d == 0 else (ci > ri)
                beta_c = gates[:, BETA_COL + col:BETA_COL + col + 1]
                g_c = gates[:, G_COL + col:G_COL + col + 1]
                g_r = gates_t[G_COL + col:G_COL + col + 1, :]
                gam_c = jnp.sum(jnp.where(incl, g_r, 0.0), axis=1, keepdims=True)
                gam_r = jnp.sum(jnp.where(incl_t, g_c, 0.0), axis=0, keepdims=True)
                decay = jnp.where(incl, jnp.exp(jnp.where(incl, gam_c - gam_r, 0.0)), 0.0)
                e_c = jnp.exp(gam_c)
                g_end = jnp.sum(g_c, axis=0, keepdims=True)
                a_scr[ch * n_solves + col][...] = jnp.where(strict, beta_c * kk * decay, 0.0)
                rhs_scr[ch * n_solves + col][...] = jnp.concatenate([v * beta_c, k * beta_c * e_c], axis=1)
                qk_ref[0, d, hh, rows, :] = qkm * decay
                qd_ref[0, d, rows, hs] = q * e_c
                kd_ref[0, d, rows, hs] = k * jnp.exp(g_end - gam_c)
                gt_ref[0, d, hh, ch * SUBLANES:(ch + 1) * SUBLANES, :] = jnp.broadcast_to(
                    jnp.exp(g_end), (SUBLANES, LANES))

    def substitute_from(done):
        lane = lax.broadcasted_iota(jnp.int32, (c - done, c), 1)

        def substitute(j8, carry):
            j = done + j8
            for sol in range(GC_CHUNKS * n_solves):
                forward = sol % n_solves < DN_HEADS
                jj = j if forward else c - 1 - j
                rs = slice(done, c) if forward else slice(0, c - done)
                xj = rhs_scr[sol][pl.ds(jj, 1), :]
                a_col = jnp.sum(jnp.where(lane == jj, a_scr[sol][rs, :], 0.0), axis=1, keepdims=True)
                rhs_scr[sol][rs, :] = rhs_scr[sol][rs, :] - a_col * xj
            return carry
        return substitute

    for done in range(0, c, SUBLANES):
        lax.fori_loop(0, SUBLANES, substitute_from(done), 0)
    for ch in range(GC_CHUNKS):
        rows = slice(ch * c, (ch + 1) * c)
        for hh in range(DN_HEADS):
            hs = slice(hh * DN_HEAD_DIM, (hh + 1) * DN_HEAD_DIM)
            for d in range(N_DIRS):
                sol = ch * n_solves + d * DN_HEADS + hh
                u_ref[0, d, rows, hs] = rhs_scr[sol][:, :DN_HEAD_DIM]
                w_ref[0, d, rows, hs] = rhs_scr[sol][:, DN_HEAD_DIM:]


def gdn_chunks(q, k, v, gates):
    b, n, _ = q.shape
    c = DN_CHUNK
    rows = GC_CHUNKS * c
    row = lambda w: pl.BlockSpec((1, rows, w), lambda i, j: (i, j, 0))
    drow = lambda w: pl.BlockSpec((1, N_DIRS, rows, w), lambda i, j: (i, 0, j, 0))
    wide = jax.ShapeDtypeStruct((b, N_DIRS, n, DN_W), jnp.float32)
    return pl.pallas_call(
        _gdn_chunk_kernel,
        grid=(b, n // rows),
        in_specs=[row(DN_W), row(DN_W), row(DN_W), row(GB_PAD)],
        out_specs=[drow(DN_W), drow(DN_W), drow(DN_W), drow(DN_W),
                   pl.BlockSpec((1, N_DIRS, DN_HEADS, rows, c), lambda i, j: (i, 0, 0, j, 0)),
                   pl.BlockSpec((1, N_DIRS, DN_HEADS, GC_CHUNKS * SUBLANES, LANES),
                                lambda i, j: (i, 0, 0, j, 0))],
        out_shape=[wide, wide, wide, wide,
                   jax.ShapeDtypeStruct((b, N_DIRS, DN_HEADS, n, c), jnp.float32),
                   jax.ShapeDtypeStruct((b, N_DIRS, DN_HEADS, n // c * SUBLANES, LANES), jnp.float32)],
        scratch_shapes=[pltpu.VMEM((c, c), jnp.float32)] * (GC_CHUNKS * N_DIRS * DN_HEADS)
                       + [pltpu.VMEM((c, 2 * DN_HEAD_DIM), jnp.float32)] * (GC_CHUNKS * N_DIRS * DN_HEADS),
        name="gdn_chunks",
    )(q, k, v, gates)


SCAN_CHUNKS = 8


def _gdn_scan_kernel(s0_ref, uf, wf, qdf, kdf, qkf, gtf, ub, wb, qdb, kdb, qkb, gtb,
                     of_ref, ob_ref, sfin_ref, s_scr):
    c = DN_CHUNK
    step = pl.program_id(2)
    n_local = uf.shape[2] // c

    @pl.when(step == 0)
    def _():
        s_scr[...] = s0_ref[0, :, 0]

    dirs = ((uf, wf, qdf, kdf, qkf, gtf, of_ref), (ub, wb, qdb, kdb, qkb, gtb, ob_ref))
    for i in range(n_local):
        for d, (u, w, qd, kd, qk, gt, o_ref) in enumerate(dirs):
            li = i if d == 0 else n_local - 1 - i
            rows = slice(li * c, (li + 1) * c)
            s = s_scr[d]
            sb = s.astype(jnp.bfloat16)
            v_new = u[0, 0, rows, :] - jnp.dot(w[0, 0, rows, :].astype(jnp.bfloat16), sb,
                                               preferred_element_type=jnp.float32)
            vb = v_new.astype(jnp.bfloat16)
            o_ref[0, rows, :] = (jnp.dot(qd[0, 0, rows, :].astype(jnp.bfloat16), sb,
                                         preferred_element_type=jnp.float32)
                                 + jnp.dot(qk[0, 0, 0, rows, :].astype(jnp.bfloat16), vb,
                                           preferred_element_type=jnp.float32))
            s_scr[d] = (s * gt[0, 0, 0, li * SUBLANES:li * SUBLANES + 1, :]
                        + lax.dot_general(kd[0, 0, rows, :].astype(jnp.bfloat16), vb, (((0,), (0,)), ((), ())),
                                          preferred_element_type=jnp.float32))

    @pl.when(step == pl.num_programs(2) - 1)
    def _():
        sfin_ref[0, :, 0] = s_scr[...]


def gdn_scan(s0, u, w, qd, kd, qk, gt):
    b, _, n, _ = u.shape
    c = DN_CHUNK
    per = min(SCAN_CHUNKS, n // c)
    nb = n // (per * c)
    rows = per * c

    def specs(d):
        blk = (lambda j: j) if d == 0 else (lambda j: nb - 1 - j)
        wide = pl.BlockSpec((1, 1, rows, DN_HEAD_DIM), lambda i, h, j: (i, d, blk(j), h))
        return [wide, wide, wide, wide,
                pl.BlockSpec((1, 1, 1, rows, c), lambda i, h, j: (i, d, h, blk(j), 0)),
                pl.BlockSpec((1, 1, 1, per * SUBLANES, LANES), lambda i, h, j: (i, d, h, blk(j), 0))]

    state_spec = pl.BlockSpec((1, N_DIRS, 1, DN_HEAD_DIM, DN_HEAD_DIM), lambda i, h, j: (i, 0, h, 0, 0))
    return pl.pallas_call(
        _gdn_scan_kernel,
        grid=(b, DN_HEADS, nb),
        in_specs=[state_spec] + specs(0) + specs(1),
        out_specs=[pl.BlockSpec((1, rows, DN_HEAD_DIM), lambda i, h, j: (i, j, h)),
                   pl.BlockSpec((1, rows, DN_HEAD_DIM), lambda i, h, j: (i, nb - 1 - j, h)),
                   state_spec],
        out_shape=[jax.ShapeDtypeStruct((b, n, DN_W), jnp.float32),
                   jax.ShapeDtypeStruct((b, n, DN_W), jnp.float32),
                   jax.ShapeDtypeStruct(s0.shape, jnp.float32)],
        scratch_shapes=[pltpu.VMEM((N_DIRS, DN_HEAD_DIM, DN_HEAD_DIM), jnp.float32)],
        name="gdn_scan",
    )(s0, u, w, qd, kd, qk, gt, u, w, qd, kd, qk, gt)


OP_ROWS = 256


def _out_proj_kernel(x_ref, attn_ref, of_ref, ob_ref, z_ref, nw_ref, w_ref, gate_ref, g_ref, b_ref,
                     sh_ref, sc_ref, x1_ref, h2_ref):
    parts = [attn_ref[0]]
    for hh in range(DN_HEADS):
        cs = slice(hh * DN_HEAD_DIM, (hh + 1) * DN_HEAD_DIM)
        o = of_ref[0, :, cs] + ob_ref[0, :, cs]
        y = o * lax.rsqrt(jnp.mean(o * o, axis=-1, keepdims=True) + LN_EPS) * nw_ref[...]
        z = z_ref[0, :, cs]
        parts.append(y * (z * jax.nn.sigmoid(z)))
    mix = jnp.concatenate(parts, axis=-1).astype(jnp.bfloat16)
    y = jnp.dot(mix, w_ref[...], preferred_element_type=jnp.float32)
    x1 = _standardize_rows(DEEPNORM_ALPHA * x_ref[0] + gate_ref[0] * y) * g_ref[...] + b_ref[...]
    x1_ref[0] = x1
    h2_ref[0] = _standardize_rows(x1) * (1.0 + sc_ref[0]) + sh_ref[0]


def out_proj(x, attn, o_f, o_b, z, dn_norm_w, w_out, gate, ln_g, ln_b, shift2, scale2):
    b, n, d = x.shape
    rows = min(OP_ROWS, n)
    row_spec = lambda w: pl.BlockSpec((1, rows, w), lambda i, j: (i, j, 0))
    per_batch = pl.BlockSpec((1, 1, d), lambda i, j: (i, 0, 0))
    const = lambda w: pl.BlockSpec((1, w), lambda i, j: (0, 0))
    return pl.pallas_call(
        _out_proj_kernel,
        grid=(b, n // rows),
        in_specs=[row_spec(d), row_spec(ATTN_Q_W), row_spec(DN_W), row_spec(DN_W), row_spec(DN_W),
                  const(DN_HEAD_DIM), pl.BlockSpec((MIX_W, d), lambda i, j: (0, 0)),
                  per_batch, const(d), const(d), per_batch, per_batch],
        out_specs=[row_spec(d), row_spec(d)],
        out_shape=[jax.ShapeDtypeStruct((b, n, d), jnp.float32)] * 2,
        name="out_proj",
    )(x, attn, o_f, o_b, z, dn_norm_w.reshape(1, DN_HEAD_DIM), w_out.astype(jnp.bfloat16),
      gate.reshape(b, 1, d), ln_g.reshape(1, d), ln_b.reshape(1, d), shift2.reshape(b, 1, d),
      scale2.reshape(b, 1, d))


PS_TOKENS = 512


PEER_PAIRS = tuple((i, j) for i in range(PEER_TOPK) for j in range(PEER_TOPK) if (i + 1) * (j + 1) <= PEER_TOPK)
PEER_PAIR_ROWS = -(-len(PEER_PAIRS) // SUBLANES) * SUBLANES


def _top16_rows(s, n_rows):
    row = lax.broadcasted_iota(jnp.int32, (n_rows, LANES), 0)
    vals, idxs = [], []
    for _ in range(PEER_TOPK):
        m = jnp.max(s, axis=0, keepdims=True)
        i = jnp.min(jnp.where(s == m, row, n_rows), axis=0, keepdims=True)
        s = jnp.where(row == i, -jnp.inf, s)
        vals.append(m)
        idxs.append(i)
    return vals, idxs


def _peer_select_kernel(x_ref, wq_ref, sk_ref, exp_ref, gate_ref, s_scr):
    q = jnp.dot(x_ref[...].astype(jnp.bfloat16), wq_ref[...].astype(jnp.bfloat16),
                preferred_element_type=jnp.float32)
    for x in range(2):
        qx = q[:, x * PEER_KEY_HALF:(x + 1) * PEER_KEY_HALF].astype(jnp.bfloat16)
        s_scr[x] = lax.dot_general(sk_ref[x].astype(jnp.bfloat16), qx, (((1,), (1,)), ((), ())),
                                   preferred_element_type=jnp.float32)

    def chunk(c, carry):
        off = pl.multiple_of(c * LANES, LANES)
        s1, i1 = _top16_rows(s_scr[0, :, pl.ds(off, LANES)], PEER_N_KEYS)
        s2, i2 = _top16_rows(s_scr[1, :, pl.ds(off, LANES)], PEER_N_KEYS)
        pad = [jnp.full((1, LANES), -jnp.inf, jnp.float32)] * (PEER_PAIR_ROWS - len(PEER_PAIRS))
        cand = jnp.concatenate([s1[i] + s2[j] for i, j in PEER_PAIRS] + pad, axis=0)
        best, pos = _top16_rows(cand, PEER_PAIR_ROWS)
        best = jnp.concatenate(best, axis=0)
        pos = jnp.concatenate(pos, axis=0)
        k1 = jnp.zeros_like(pos)
        k2 = jnp.zeros_like(pos)
        for r, (i, j) in enumerate(PEER_PAIRS):
            hit = pos == r
            k1 = jnp.where(hit, i1[i], k1)
            k2 = jnp.where(hit, i2[j], k2)
        exp_ref[:, pl.ds(off, LANES)] = k1 * PEER_N_KEYS + k2
        e = jnp.exp(best - best[0:1, :])
        gate_ref[:, pl.ds(off, LANES)] = e / jnp.sum(e, axis=0, keepdims=True)
        return carry

    lax.fori_loop(0, PS_TOKENS // LANES, chunk, 0)


def peer_select(x, w_query, sub_keys):
    t, d = x.shape
    hw = 2 * PEER_KEY_HALF
    return pl.pallas_call(
        _peer_select_kernel,
        grid=(PEER_HEADS, t // PS_TOKENS),
        in_specs=[
            pl.BlockSpec((PS_TOKENS, d), lambda h, i: (i, 0)),
            pl.BlockSpec((d, hw), lambda h, i: (0, h)),
            pl.BlockSpec((2, PEER_N_KEYS, PEER_KEY_HALF), lambda h, i: (0, 0, 0)),
        ],
        out_specs=[
            pl.BlockSpec((PEER_TOPK, PS_TOKENS), lambda h, i: (h, i)),
            pl.BlockSpec((PEER_TOPK, PS_TOKENS), lambda h, i: (h, i)),
        ],
        out_shape=[
            jax.ShapeDtypeStruct((PEER_HEADS * PEER_TOPK, t), jnp.int32),
            jax.ShapeDtypeStruct((PEER_HEADS * PEER_TOPK, t), jnp.float32),
        ],
        scratch_shapes=[pltpu.VMEM((2, PEER_N_KEYS, PS_TOKENS), jnp.float32)],
        name="peer_select",
    )(x, w_query, sub_keys)


PEER_SEL = PEER_HEADS * PEER_TOPK
PG_TOKENS = 128
PG_SLOTS = 8
PG_AHEAD = PG_SLOTS - 1
N_DMA_QUEUES = 2
_SQRT_HALF = 0.7071067811865476


def _unpack_bf16_pair(words):
    low = lax.bitcast_convert_type(words << 16, jnp.float32)
    high = lax.bitcast_convert_type(words & jnp.uint32(0xFFFF0000), jnp.float32)
    return low, high


def _peer_eval_kernel(idx_ref, x_ref, gt_ref, uv_hbm, o_ref, buf, sem):
    n_chunks = x_ref.shape[-1] // LANES
    n_packed = n_chunks // 2

    def issue(t, slot):
        for e in range(PEER_SEL):
            row = idx_ref[t * PEER_SEL + e]
            pltpu.make_async_copy(uv_hbm.at[row], buf.at[slot, :, e, :], sem.at[slot]).start(
                priority=e % N_DMA_QUEUES)

    def wait(slot):
        pltpu.make_async_copy(buf.at[slot], buf.at[slot], sem.at[slot]).wait()

    @pl.when(pl.program_id(0) == 0)
    def _():
        for s in range(PG_AHEAD):
            issue(s, s)

    lane = lax.broadcasted_iota(jnp.int32, (PEER_SEL, PG_TOKENS), 1)

    def group(g, carry):
        for s in range(PG_SLOTS):
            t = g * PG_SLOTS + s
            issue(t + PG_AHEAD, (s + PG_AHEAD) % PG_SLOTS)
            wait(s)
            x = x_ref[pl.ds(t, 1), :]
            p = None
            for c in range(n_packed):
                low, high = _unpack_bf16_pair(buf[s, c])
                term = (low * x[:, c * LANES:(c + 1) * LANES]
                        + high * x[:, (n_packed + c) * LANES:(n_packed + c + 1) * LANES])
                p = term if p is None else p + term
            a = jnp.sum(p, axis=-1, keepdims=True)
            act = 0.5 * a * (1.0 + lax.erf(a * _SQRT_HALF))
            gate = jnp.sum(jnp.where(lane == t, gt_ref[...], 0.0), axis=-1, keepdims=True)
            w = gate * act
            lows, highs = [], []
            for c in range(n_packed):
                low, high = _unpack_bf16_pair(buf[s, n_packed + c])
                lows.append(jnp.sum(w * low, axis=0, keepdims=True))
                highs.append(jnp.sum(w * high, axis=0, keepdims=True))
            o_ref[pl.ds(t, 1), :] = jnp.concatenate(lows + highs, axis=-1)
        return carry

    lax.fori_loop(0, PG_TOKENS // PG_SLOTS, group, 0)

    @pl.when(pl.program_id(0) == pl.num_programs(0) - 1)
    def _():
        for s in range(PG_AHEAD):
            wait(s)


def peer_eval(x, expert, gate_t, uv):
    t, d = x.shape
    blocks = expert.reshape(t // PG_TOKENS, PG_TOKENS, PEER_SEL)
    ahead = jnp.roll(blocks[:, :PG_SLOTS], -1, axis=0)
    expert = jnp.concatenate([blocks, ahead], axis=1).reshape(-1)
    return pl.pallas_call(
        _peer_eval_kernel,
        grid=(t // PG_TOKENS,),
        in_specs=[
            pl.BlockSpec(((PG_TOKENS + PG_SLOTS) * PEER_SEL,), lambda i: (i,), memory_space=pltpu.SMEM),
            pl.BlockSpec((PG_TOKENS, d), lambda i: (i, 0)),
            pl.BlockSpec((PEER_SEL, PG_TOKENS), lambda i: (0, i)),
            pl.BlockSpec(memory_space=pl.ANY),
        ],
        out_specs=pl.BlockSpec((PG_TOKENS, d), lambda i: (i, 0)),
        out_shape=jax.ShapeDtypeStruct((t, d), jnp.float32),
        scratch_shapes=[
            pltpu.VMEM((PG_SLOTS, d // LANES, PEER_SEL, LANES), jnp.uint32),
            pltpu.SemaphoreType.DMA((PG_SLOTS,)),
        ],
        compiler_params=pltpu.CompilerParams(dimension_semantics=("arbitrary",)),
        name="peer_eval",
    )(expert, x, gate_t, uv)


def pack_bf16_halves(t):
    bits = lax.bitcast_convert_type(t.astype(jnp.bfloat16), jnp.uint16).astype(jnp.uint32)
    half = t.shape[1] // 2
    return bits[:, :half] | (bits[:, half:] << 16)


def peer(h, w_query, sub_keys, u_tab, v_tab):
    b, n, d = h.shape
    x2 = h.reshape(b * n, d)
    expert_t, gate_t = peer_select(x2, w_query, sub_keys)
    uv = jnp.concatenate([pack_bf16_halves(u_tab), pack_bf16_halves(v_tab)], axis=-1)
    uv = uv.reshape(PEER_EXPERTS, d // LANES, LANES)
    out = peer_eval(x2, expert_t.T, gate_t, uv)
    return out.reshape(b, n, d)


LN_ROWS = 512


def _res_ln_kernel(x_ref, y_ref, gate_ref, g_ref, b_ref, o_ref):
    t = DEEPNORM_ALPHA * x_ref[0] + gate_ref[0] * y_ref[0]
    o_ref[0] = _standardize_rows(t) * g_ref[...] + b_ref[...]


def res_ln(x, y, gate, g, b):
    bsz, n, d = x.shape
    return pl.pallas_call(
        _res_ln_kernel,
        grid=(bsz, n // LN_ROWS),
        in_specs=[
            pl.BlockSpec((1, LN_ROWS, d), lambda i, j: (i, j, 0)),
            pl.BlockSpec((1, LN_ROWS, d), lambda i, j: (i, j, 0)),
            pl.BlockSpec((1, 1, d), lambda i, j: (i, 0, 0)),
            pl.BlockSpec((1, d), lambda i, j: (0, 0)),
            pl.BlockSpec((1, d), lambda i, j: (0, 0)),
        ],
        out_specs=pl.BlockSpec((1, LN_ROWS, d), lambda i, j: (i, j, 0)),
        out_shape=jax.ShapeDtypeStruct(x.shape, x.dtype),
        name="res_ln",
    )(x, y, gate.reshape(bsz, 1, d), g.reshape(1, d), b.reshape(1, d))


def delta_net(qkv_d, gb, conv_w, a_log, dt_bias, s0):
    q, k, v, gates = gdn_prep(qkv_d, gb, conv_w, a_log, dt_bias)
    return gdn_scan(s0, *gdn_chunks(q, k, v, gates))


def kernel(x, c, ctx, c_ctx, w_ada, b_ada, w_in, conv_w, a_log, dt_bias, sink, dn_norm_w, w_out,
           ln1_g, ln1_b, peer_wq, peer_sub_keys, peer_u, peer_v, ln2_g, ln2_b):
    assert w_ada.shape[0] == DEPTH == 1
    l = 0
    b, n, d = x.shape
    m = ctx.shape[1]
    cond = jnp.concatenate([c, c_ctx[None], jnp.zeros((SUBLANES - b - 1, d), c.dtype)], axis=0)
    mod = ada_mod(cond, w_ada[l], b_ada[l])
    sh1, sc1, gt1, sh2, sc2, gt2 = jnp.split(mod[:b], 6, axis=-1)
    csh1, csc1 = (jnp.broadcast_to(t, (b, d)) for t in jnp.split(mod[b], 6, axis=-1)[:2])
    w_pad = jnp.pad(w_in[l], ((0, 0), (0, IN_PAD_W - D_IN))).astype(jnp.bfloat16)
    qkv, dn, z, gb = in_proj(x, sh1, sc1, w_pad, *rope_tables(n, True))
    qkv_c, dn_c, _, gb_c = in_proj(ctx, csh1, csc1, w_pad, *rope_tables(m, False))
    attn = window_attention(qkv, qkv_c, sink[l])
    s0 = jnp.zeros((b, N_DIRS, DN_HEADS, DN_HEAD_DIM, DN_HEAD_DIM), jnp.float32)
    _, _, s_ctx = delta_net(dn_c, gb_c, conv_w[l], a_log[l], dt_bias[l], s0)
    o_f, o_b, _ = delta_net(dn, gb, conv_w[l], a_log[l], dt_bias[l], s_ctx)
    x1, h2 = out_proj(x, attn, o_f, o_b, z, dn_norm_w[l], w_out[l], gt1, ln1_g[l], ln1_b[l], sh2, sc2)
    p = peer(h2, peer_wq[l], peer_sub_keys[l], peer_u[l], peer_v[l])
    return res_ln(x1, p, gt2, ln2_g[l], ln2_b[l])
```

```python
import jax
import jax.numpy as jnp
from jax import lax
from jax.experimental import pallas as pl
from jax.experimental.pallas import tpu as pltpu

D_MODEL = 1024
DEPTH = 1
GRID_W = 64

ATTN_HEADS = 8
ATTN_KV_HEADS = 2
ATTN_GROUP = ATTN_HEADS // ATTN_KV_HEADS
ATTN_HEAD_DIM = 64
WINDOW = 128
ATTN_BLOCK = 128
ROPE_BASE = 10000.0
DN_HEADS = 4
DN_HEAD_DIM = 128
DN_CONV = 5
DN_CHUNK = 64
ATTN_Q_W = ATTN_HEADS * ATTN_HEAD_DIM
ATTN_KV_W = ATTN_KV_HEADS * ATTN_HEAD_DIM
DN_W = DN_HEADS * DN_HEAD_DIM
MIX_W = ATTN_Q_W + DN_W
D_IN = ATTN_Q_W + 2 * ATTN_KV_W + 4 * DN_W + 4 * DN_HEADS
PEER_HEADS = 8
PEER_N_KEYS = 128
PEER_EXPERTS = PEER_N_KEYS ** 2
PEER_KEY_HALF = 128
PEER_TOPK = 16
DEEPNORM_ALPHA = (2 * DEPTH) ** 0.25
LN_EPS = 1e-6
NEG_INF = -1e30

LANES = 128
SUBLANES = 8


def _standardize_rows(t):
    mu = jnp.mean(t, axis=-1, keepdims=True)
    tc = t - mu
    var = jnp.mean(tc * tc, axis=-1, keepdims=True)
    return tc * lax.rsqrt(var + LN_EPS)


ADA_COLS = 512


def _ada_kernel(c_ref, w_ref, b_ref, o_ref):
    c = c_ref[...]
    s = c * jax.nn.sigmoid(c)
    o_ref[...] = jnp.dot(s.astype(jnp.bfloat16), w_ref[...].astype(jnp.bfloat16),
                         preferred_element_type=jnp.float32) + b_ref[...]


def ada_mod(c_rows, w_ada, b_ada):
    r, d = c_rows.shape
    n = w_ada.shape[1]
    return pl.pallas_call(
        _ada_kernel,
        grid=(n // ADA_COLS,),
        in_specs=[pl.BlockSpec((r, d), lambda j: (0, 0)),
                  pl.BlockSpec((d, ADA_COLS), lambda j: (0, j)),
                  pl.BlockSpec((1, ADA_COLS), lambda j: (0, j))],
        out_specs=pl.BlockSpec((r, ADA_COLS), lambda j: (0, j)),
        out_shape=jax.ShapeDtypeStruct((r, n), jnp.float32),
        name="ada_mod",
    )(c_rows, w_ada, b_ada.reshape(1, n))


IP_ROWS = 256
ATTN_QKV_W = ATTN_Q_W + 2 * ATTN_KV_W
DN_QKV_W = 3 * DN_W
GB_PAD = LANES
IN_PAD_W = ATTN_QKV_W + DN_QKV_W + DN_W + GB_PAD
ROPE_W = ATTN_Q_W + ATTN_KV_W
ROPE_PAIR = ATTN_HEAD_DIM // 4


def _in_proj_kernel(x_ref, shift_ref, scale_ref, w_ref, cos_ref, sin_ref, qkv_ref, dn_ref, z_ref, gb_ref):
    h = _standardize_rows(x_ref[0]) * (1.0 + scale_ref[0]) + shift_ref[0]
    y = jnp.dot(h.astype(jnp.bfloat16), w_ref[...], preferred_element_type=jnp.float32)
    cos = cos_ref[...]
    sin = sin_ref[...]
    lane = lax.broadcasted_iota(jnp.int32, cos.shape, 1)
    first = (lane // ROPE_PAIR) % 2 == 0
    for c in range(ROPE_W // LANES):
        t = y[:, c * LANES:(c + 1) * LANES]
        partner = jnp.where(first, pltpu.roll(t, LANES - ROPE_PAIR, axis=1), pltpu.roll(t, ROPE_PAIR, axis=1))
        qkv_ref[0, :, c * LANES:(c + 1) * LANES] = t * cos + partner * sin
    qkv_ref[0, :, ROPE_W:] = y[:, ROPE_W:ATTN_QKV_W]
    dn_ref[0] = y[:, ATTN_QKV_W:ATTN_QKV_W + DN_QKV_W]
    z_ref[0] = y[:, ATTN_QKV_W + DN_QKV_W:ATTN_QKV_W + DN_QKV_W + DN_W]
    gb_ref[0] = y[:, ATTN_QKV_W + DN_QKV_W + DN_W:]


def rope_tables(n, rotate):
    if not rotate:
        return jnp.ones((n, LANES), jnp.float32), jnp.zeros((n, LANES), jnp.float32)
    freqs = ROPE_BASE ** (-jnp.arange(ROPE_PAIR, dtype=jnp.float32) / ROPE_PAIR)
    pos = jnp.arange(n)
    ang_r = (pos // GRID_W).astype(jnp.float32)[:, None] * freqs
    ang_c = (pos % GRID_W).astype(jnp.float32)[:, None] * freqs
    cos = jnp.concatenate([jnp.cos(ang_r), jnp.cos(ang_r), jnp.cos(ang_c), jnp.cos(ang_c)], -1)
    sin = jnp.concatenate([-jnp.sin(ang_r), jnp.sin(ang_r), -jnp.sin(ang_c), jnp.sin(ang_c)], -1)
    reps = LANES // ATTN_HEAD_DIM
    return jnp.tile(cos, (1, reps)), jnp.tile(sin, (1, reps))


def in_proj(x, shift, scale, w_pad, cos, sin):
    b, n, d = x.shape
    rows = min(IP_ROWS, n)
    row_spec = lambda w: pl.BlockSpec((1, rows, w), lambda i, j: (i, j, 0))
    per_batch = pl.BlockSpec((1, 1, d), lambda i, j: (i, 0, 0))
    return pl.pallas_call(
        _in_proj_kernel,
        grid=(b, n // rows),
        in_specs=[row_spec(d), per_batch, per_batch,
                  pl.BlockSpec((d, IN_PAD_W), lambda i, j: (0, 0)),
                  pl.BlockSpec((rows, LANES), lambda i, j: (j, 0)),
                  pl.BlockSpec((rows, LANES), lambda i, j: (j, 0))],
        out_specs=[row_spec(ATTN_QKV_W), row_spec(DN_QKV_W), row_spec(DN_W), row_spec(GB_PAD)],
        out_shape=[jax.ShapeDtypeStruct((b, n, w), jnp.float32)
                   for w in (ATTN_QKV_W, DN_QKV_W, DN_W, GB_PAD)],
        name="in_proj",
    )(x, shift.reshape(b, 1, d), scale.reshape(b, 1, d), w_pad, cos, sin)


def _attn_kernel(sink_ref, q_ref, kp_ref, kc_ref, kn_ref, vp_ref, vc_ref, vn_ref, kx_ref, vx_ref, o_ref):
    j = pl.program_id(1)
    nblk = pl.num_programs(1)
    blk = ATTN_BLOCK
    dh = ATTN_HEAD_DIM
    scale = ATTN_HEAD_DIM ** -0.5
    q_off = lax.broadcasted_iota(jnp.int32, (blk, 3 * blk), 0)
    k_off = lax.broadcasted_iota(jnp.int32, (blk, 3 * blk), 1) - blk
    k_abs = j * blk + k_off
    mask = (jnp.abs(q_off - k_off) <= WINDOW) & (k_abs >= 0) & (k_abs < nblk * blk)
    mask = jnp.concatenate([mask] * ATTN_GROUP, axis=0)
    k_loc = jnp.concatenate([kp_ref[0], kc_ref[0], kn_ref[0]], axis=0).astype(jnp.bfloat16)
    v_loc = jnp.concatenate([vp_ref[0], vc_ref[0], vn_ref[0]], axis=0).astype(jnp.bfloat16)
    k_ctx = kx_ref[0].astype(jnp.bfloat16)
    v_ctx = vx_ref[0].astype(jnp.bfloat16)
    q_all = q_ref[0].astype(jnp.bfloat16)
    contract_last = (((1,), (1,)), ((), ()))
    for kvh in range(ATTN_KV_HEADS):
        q = jnp.concatenate([q_all[:, (kvh * ATTN_GROUP + g) * dh:(kvh * ATTN_GROUP + g + 1) * dh]
                             for g in range(ATTN_GROUP)], axis=0)
        ks = slice(kvh * dh, (kvh + 1) * dh)
        s_loc = lax.dot_general(q, k_loc[:, ks], contract_last, preferred_element_type=jnp.float32) * scale
        s_loc = jnp.where(mask, s_loc, NEG_INF)
        s_ctx = lax.dot_general(q, k_ctx[:, ks], contract_last, preferred_element_type=jnp.float32) * scale
        snk = jnp.concatenate([jnp.full((blk, 1), sink_ref[kvh * ATTN_GROUP + g], jnp.float32)
                               for g in range(ATTN_GROUP)], axis=0)
        m = jnp.maximum(jnp.maximum(jnp.max(s_loc, axis=-1, keepdims=True),
                                    jnp.max(s_ctx, axis=-1, keepdims=True)), snk)
        p_loc = jnp.exp(s_loc - m)
        p_ctx = jnp.exp(s_ctx - m)
        denom = (jnp.sum(p_loc, axis=-1, keepdims=True) + jnp.sum(p_ctx, axis=-1, keepdims=True)
                 + jnp.exp(snk - m))
        inv = 1.0 / denom
        o = (jnp.dot((p_loc * inv).astype(jnp.bfloat16), v_loc[:, ks], preferred_element_type=jnp.float32)
             + jnp.dot((p_ctx * inv).astype(jnp.bfloat16), v_ctx[:, ks], preferred_element_type=jnp.float32))
        for g in range(ATTN_GROUP):
            hh = kvh * ATTN_GROUP + g
            o_ref[0, :, hh * dh:(hh + 1) * dh] = o[g * blk:(g + 1) * blk]


def window_attention(qkv, kv_ctx, sink):
    b, n, _ = qkv.shape
    m = kv_ctx.shape[1]
    nblk = n // ATTN_BLOCK
    k_col = ATTN_Q_W // ATTN_KV_W
    v_col = k_col + 1

    def neighbour(col, d):
        return pl.BlockSpec((1, ATTN_BLOCK, ATTN_KV_W),
                            lambda i, j: (i, jnp.clip(j + d, 0, nblk - 1), col))

    return pl.pallas_call(
        _attn_kernel,
        grid=(b, nblk),
        in_specs=[
            pl.BlockSpec(memory_space=pltpu.SMEM),
            pl.BlockSpec((1, ATTN_BLOCK, ATTN_Q_W), lambda i, j: (i, j, 0)),
            neighbour(k_col, -1), neighbour(k_col, 0), neighbour(k_col, 1),
            neighbour(v_col, -1), neighbour(v_col, 0), neighbour(v_col, 1),
            pl.BlockSpec((1, m, ATTN_KV_W), lambda i, j: (i, 0, k_col)),
            pl.BlockSpec((1, m, ATTN_KV_W), lambda i, j: (i, 0, v_col)),
        ],
        out_specs=pl.BlockSpec((1, ATTN_BLOCK, ATTN_Q_W), lambda i, j: (i, j, 0)),
        out_shape=jax.ShapeDtypeStruct((b, n, ATTN_Q_W), jnp.float32),
        name="window_attention",
    )(sink, qkv, qkv, qkv, qkv, qkv, qkv, qkv, kv_ctx, kv_ctx)


GP_ROWS = 256
HALO = SUBLANES
N_DIRS = 2
N_GATES = N_DIRS * DN_HEADS
BETA_COL = 0
G_COL = N_GATES


def _gdn_prep_kernel(prev_ref, cur_ref, next_ref, gb_ref, cw_ref, alog_ref, dt_ref,
                     q_ref, k_ref, v_ref, gate_ref):
    j = pl.program_id(1)
    rows = cur_ref.shape[1]
    prev = jnp.where(j > 0, prev_ref[0], 0.0)
    nxt = jnp.where(j < pl.num_programs(1) - 1, next_ref[0], 0.0)
    xe = jnp.concatenate([prev, cur_ref[0], nxt], axis=0)
    pad = DN_CONV // 2
    acc = xe[HALO - pad:HALO - pad + rows] * cw_ref[0:1, :]
    for t in range(1, DN_CONV):
        acc = acc + xe[HALO - pad + t:HALO - pad + t + rows] * cw_ref[t:t + 1, :]
    y = acc * jax.nn.sigmoid(acc)
    for hh in range(DN_HEADS):
        cs = slice(hh * DN_HEAD_DIM, (hh + 1) * DN_HEAD_DIM)
        q = y[:, hh * DN_HEAD_DIM:(hh + 1) * DN_HEAD_DIM]
        k = y[:, DN_W + hh * DN_HEAD_DIM:DN_W + (hh + 1) * DN_HEAD_DIM]
        q_ref[0, :, cs] = q * lax.rsqrt(jnp.sum(q * q, axis=-1, keepdims=True) + LN_EPS) * DN_HEAD_DIM ** -0.5
        k_ref[0, :, cs] = k * lax.rsqrt(jnp.sum(k * k, axis=-1, keepdims=True) + LN_EPS)
    v_ref[0] = y[:, 2 * DN_W:]
    gb = gb_ref[0]
    lane = lax.broadcasted_iota(jnp.int32, gb.shape, 1)
    a = gb + dt_ref[...]
    softplus = jnp.maximum(a, 0.0) + jnp.log(1.0 + jnp.exp(-jnp.abs(a)))
    gate_ref[0] = jnp.where(lane < N_GATES, jax.nn.sigmoid(gb), -jnp.exp(alog_ref[...]) * softplus)


def gdn_prep(qkv_d, gb, conv_w, a_log, dt_bias):
    b, n, w = qkv_d.shape
    rows = min(GP_ROWS, n)
    hb = rows // HALO
    last = n // HALO - 1
    lane_vec = lambda p: jnp.pad(p.reshape(-1), (N_GATES, GB_PAD - 2 * N_GATES)).reshape(1, GB_PAD)
    row_spec = lambda width: pl.BlockSpec((1, rows, width), lambda i, j: (i, j, 0))
    return pl.pallas_call(
        _gdn_prep_kernel,
        grid=(b, n // rows),
        in_specs=[
            pl.BlockSpec((1, HALO, w), lambda i, j: (i, jnp.maximum(j * hb - 1, 0), 0)),
            row_spec(w),
            pl.BlockSpec((1, HALO, w), lambda i, j: (i, jnp.minimum((j + 1) * hb, last), 0)),
            row_spec(GB_PAD),
            pl.BlockSpec((HALO, w), lambda i, j: (0, 0)),
            pl.BlockSpec((1, GB_PAD), lambda i, j: (0, 0)),
            pl.BlockSpec((1, GB_PAD), lambda i, j: (0, 0)),
        ],
        out_specs=[row_spec(DN_W), row_spec(DN_W), row_spec(DN_W), row_spec(GB_PAD)],
        out_shape=[jax.ShapeDtypeStruct((b, n, DN_W), jnp.float32)] * 3
                  + [jax.ShapeDtypeStruct((b, n, GB_PAD), jnp.float32)],
        name="gdn_prep",
    )(qkv_d, qkv_d, qkv_d, gb, jnp.pad(conv_w, ((0, HALO - DN_CONV), (0, 0))), lane_vec(a_log), lane_vec(dt_bias))


GC_CHUNKS = 2


def _bf16_abt(a, b):
    return lax.dot_general(a.astype(jnp.bfloat16), b.astype(jnp.bfloat16), (((1,), (1,)), ((), ())),
                           preferred_element_type=jnp.float32)


def _gdn_chunk_kernel(q_ref, k_ref, v_ref, gate_ref, u_ref, w_ref, qd_ref, kd_ref, qk_ref, gt_ref,
                      *scratch):
    c = DN_CHUNK
    n_solves = N_DIRS * DN_HEADS
    a_scr = scratch[:GC_CHUNKS * n_solves]
    rhs_scr = scratch[GC_CHUNKS * n_solves:]
    ri = lax.broadcasted_iota(jnp.int32, (c, c), 0)
    ci = lax.broadcasted_iota(jnp.int32, (c, c), 1)
    for ch in range(GC_CHUNKS):
        rows = slice(ch * c, (ch + 1) * c)
        gates = gate_ref[0, rows, :]
        gates_t = gates.T
        for hh in range(DN_HEADS):
            hs = slice(hh * DN_HEAD_DIM, (hh + 1) * DN_HEAD_DIM)
            q = q_ref[0, rows, hs]
            k = k_ref[0, rows, hs]
            v = v_ref[0, rows, hs]
            kk = _bf16_abt(k, k)
            qkm = _bf16_abt(q, k)
            for d in range(N_DIRS):
                col = d * DN_HEADS + hh
                incl = (ci <= ri) if d == 0 else (ci >= ri)
                incl_t = (ri <= ci) if d == 0 else (ri >= ci)
                strict = (ci < ri) if d == 0 else (ci > ri)
                beta_c = gates[:, BETA_COL + col:BETA_COL + col + 1]
                g_c = gates[:, G_COL + col:G_COL + col + 1]
                g_r = gates_t[G_COL + col:G_COL + col + 1, :]
                gam_c = jnp.sum(jnp.where(incl, g_r, 0.0), axis=1, keepdims=True)
                gam_r = jnp.sum(jnp.where(incl_t, g_c, 0.0), axis=0, keepdims=True)
                decay = jnp.where(incl, jnp.exp(jnp.where(incl, gam_c - gam_r, 0.0)), 0.0)
                e_c = jnp.exp(gam_c)
                g_end = jnp.sum(g_c, axis=0, keepdims=True)
                a_scr[ch * n_solves + col][...] = jnp.where(strict, beta_c * kk * decay, 0.0)
                rhs_scr[ch * n_solves + col][...] = jnp.concatenate([v * beta_c, k * beta_c * e_c], axis=1)
                qk_ref[0, d, hh, rows, :] = qkm * decay
                qd_ref[0, d, rows, hs] = q * e_c
                kd_ref[0, d, rows, hs] = k * jnp.exp(g_end - gam_c)
                gt_ref[0, d, hh, ch * SUBLANES:(ch + 1) * SUBLANES, :] = jnp.broadcast_to(
                    jnp.exp(g_end), (SUBLANES, LANES))

    def substitute_from(done):
        lane = lax.broadcasted_iota(jnp.int32, (c - done, c), 1)

        def substitute(j8, carry):
            j = done + j8
            for sol in range(GC_CHUNKS * n_solves):
                forward = sol % n_solves < DN_HEADS
                jj = j if forward else c - 1 - j
                rs = slice(done, c) if forward else slice(0, c - done)
                xj = rhs_scr[sol][pl.ds(jj, 1), :]
                a_col = jnp.sum(jnp.where(lane == jj, a_scr[sol][rs, :], 0.0), axis=1, keepdims=True)
                rhs_scr[sol][rs, :] = rhs_scr[sol][rs, :] - a_col * xj
            return carry
        return substitute

    for done in range(0, c, SUBLANES):
        lax.fori_loop(0, SUBLANES, substitute_from(done), 0)
    for ch in range(GC_CHUNKS):
        rows = slice(ch * c, (ch + 1) * c)
        for hh in range(DN_HEADS):
            hs = slice(hh * DN_HEAD_DIM, (hh + 1) * DN_HEAD_DIM)
            for d in range(N_DIRS):
                sol = ch * n_solves + d * DN_HEADS + hh
                u_ref[0, d, rows, hs] = rhs_scr[sol][:, :DN_HEAD_DIM]
                w_ref[0, d, rows, hs] = rhs_scr[sol][:, DN_HEAD_DIM:]


def gdn_chunks(q, k, v, gates):
    b, n, _ = q.shape
    c = DN_CHUNK
    rows = GC_CHUNKS * c
    row = lambda w: pl.BlockSpec((1, rows, w), lambda i, j: (i, j, 0))
    drow = lambda w: pl.BlockSpec((1, N_DIRS, rows, w), lambda i, j: (i, 0, j, 0))
    wide = jax.ShapeDtypeStruct((b, N_DIRS, n, DN_W), jnp.float32)
    return pl.pallas_call(
        _gdn_chunk_kernel,
        grid=(b, n // rows),
        in_specs=[row(DN_W), row(DN_W), row(DN_W), row(GB_PAD)],
        out_specs=[drow(DN_W), drow(DN_W), drow(DN_W), drow(DN_W),
                   pl.BlockSpec((1, N_DIRS, DN_HEADS, rows, c), lambda i, j: (i, 0, 0, j, 0)),
                   pl.BlockSpec((1, N_DIRS, DN_HEADS, GC_CHUNKS * SUBLANES, LANES),
                                lambda i, j: (i, 0, 0, j, 0))],
        out_shape=[wide, wide, wide, wide,
                   jax.ShapeDtypeStruct((b, N_DIRS, DN_HEADS, n, c), jnp.float32),
                   jax.ShapeDtypeStruct((b, N_DIRS, DN_HEADS, n // c * SUBLANES, LANES), jnp.float32)],
        scratch_shapes=[pltpu.VMEM((c, c), jnp.float32)] * (GC_CHUNKS * N_DIRS * DN_HEADS)
                       + [pltpu.VMEM((c, 2 * DN_HEAD_DIM), jnp.float32)] * (GC_CHUNKS * N_DIRS * DN_HEADS),
        name="gdn_chunks",
    )(q, k, v, gates)


SCAN_CHUNKS = 8


def _gdn_scan_kernel(s0_ref, uf, wf, qdf, kdf, qkf, gtf, ub, wb, qdb, kdb, qkb, gtb,
                     of_ref, ob_ref, sfin_ref, s_scr):
    c = DN_CHUNK
    step = pl.program_id(2)
    n_local = uf.shape[2] // c

    @pl.when(step == 0)
    def _():
        s_scr[...] = s0_ref[0, :, 0]

    dirs = ((uf, wf, qdf, kdf, qkf, gtf, of_ref), (ub, wb, qdb, kdb, qkb, gtb, ob_ref))
    for i in range(n_local):
        for d, (u, w, qd, kd, qk, gt, o_ref) in enumerate(dirs):
            li = i if d == 0 else n_local - 1 - i
            rows = slice(li * c, (li + 1) * c)
            s = s_scr[d]
            sb = s.astype(jnp.bfloat16)
            v_new = u[0, 0, rows, :] - jnp.dot(w[0, 0, rows, :].astype(jnp.bfloat16), sb,
                                               preferred_element_type=jnp.float32)
            vb = v_new.astype(jnp.bfloat16)
            o_ref[0, rows, :] = (jnp.dot(qd[0, 0, rows, :].astype(jnp.bfloat16), sb,
                                         preferred_element_type=jnp.float32)
                                 + jnp.dot(qk[0, 0, 0, rows, :].astype(jnp.bfloat16), vb,
                                           preferred_element_type=jnp.float32))
            s_scr[d] = (s * gt[0, 0, 0, li * SUBLANES:li * SUBLANES + 1, :]
                        + lax.dot_general(kd[0, 0, rows, :].astype(jnp.bfloat16), vb, (((0,), (0,)), ((), ())),
                                          preferred_element_type=jnp.float32))

    @pl.when(step == pl.num_programs(2) - 1)
    def _():
        sfin_ref[0, :, 0] = s_scr[...]


def gdn_scan(s0, u, w, qd, kd, qk, gt):
    b, _, n, _ = u.shape
    c = DN_CHUNK
    per = min(SCAN_CHUNKS, n // c)
    nb = n // (per * c)
    rows = per * c

    def specs(d):
        blk = (lambda j: j) if d == 0 else (lambda j: nb - 1 - j)
        wide = pl.BlockSpec((1, 1, rows, DN_HEAD_DIM), lambda i, h, j: (i, d, blk(j), h))
        return [wide, wide, wide, wide,
                pl.BlockSpec((1, 1, 1, rows, c), lambda i, h, j: (i, d, h, blk(j), 0)),
                pl.BlockSpec((1, 1, 1, per * SUBLANES, LANES), lambda i, h, j: (i, d, h, blk(j), 0))]

    state_spec = pl.BlockSpec((1, N_DIRS, 1, DN_HEAD_DIM, DN_HEAD_DIM), lambda i, h, j: (i, 0, h, 0, 0))
    return pl.pallas_call(
        _gdn_scan_kernel,
        grid=(b, DN_HEADS, nb),
        in_specs=[state_spec] + specs(0) + specs(1),
        out_specs=[pl.BlockSpec((1, rows, DN_HEAD_DIM), lambda i, h, j: (i, j, h)),
                   pl.BlockSpec((1, rows, DN_HEAD_DIM), lambda i, h, j: (i, nb - 1 - j, h)),
                   state_spec],
        out_shape=[jax.ShapeDtypeStruct((b, n, DN_W), jnp.float32),
                   jax.ShapeDtypeStruct((b, n, DN_W), jnp.float32),
                   jax.ShapeDtypeStruct(s0.shape, jnp.float32)],
        scratch_shapes=[pltpu.VMEM((N_DIRS, DN_HEAD_DIM, DN_HEAD_DIM), jnp.float32)],
        name="gdn_scan",
    )(s0, u, w, qd, kd, qk, gt, u, w, qd, kd, qk, gt)


OP_ROWS = 256


def _out_proj_kernel(x_ref, attn_ref, of_ref, ob_ref, z_ref, nw_ref, w_ref, gate_ref, g_ref, b_ref,
                     sh_ref, sc_ref, x1_ref, h2_ref):
    parts = [attn_ref[0]]
    for hh in range(DN_HEADS):
        cs = slice(hh * DN_HEAD_DIM, (hh + 1) * DN_HEAD_DIM)
        o = of_ref[0, :, cs] + ob_ref[0, :, cs]
        y = o * lax.rsqrt(jnp.mean(o * o, axis=-1, keepdims=True) + LN_EPS) * nw_ref[...]
        z = z_ref[0, :, cs]
        parts.append(y * (z * jax.nn.sigmoid(z)))
    mix = jnp.concatenate(parts, axis=-1).astype(jnp.bfloat16)
    y = jnp.dot(mix, w_ref[...], preferred_element_type=jnp.float32)
    x1 = _standardize_rows(DEEPNORM_ALPHA * x_ref[0] + gate_ref[0] * y) * g_ref[...] + b_ref[...]
    x1_ref[0] = x1
    h2_ref[0] = _standardize_rows(x1) * (1.0 + sc_ref[0]) + sh_ref[0]


def out_proj(x, attn, o_f, o_b, z, dn_norm_w, w_out, gate, ln_g, ln_b, shift2, scale2):
    b, n, d = x.shape
    rows = min(OP_ROWS, n)
    row_spec = lambda w: pl.BlockSpec((1, rows, w), lambda i, j: (i, j, 0))
    per_batch = pl.BlockSpec((1, 1, d), lambda i, j: (i, 0, 0))
    const = lambda w: pl.BlockSpec((1, w), lambda i, j: (0, 0))
    return pl.pallas_call(
        _out_proj_kernel,
        grid=(b, n // rows),
        in_specs=[row_spec(d), row_spec(ATTN_Q_W), row_spec(DN_W), row_spec(DN_W), row_spec(DN_W),
                  const(DN_HEAD_DIM), pl.BlockSpec((MIX_W, d), lambda i, j: (0, 0)),
                  per_batch, const(d), const(d), per_batch, per_batch],
        out_specs=[row_spec(d), row_spec(d)],
        out_shape=[jax.ShapeDtypeStruct((b, n, d), jnp.float32)] * 2,
        name="out_proj",
    )(x, attn, o_f, o_b, z, dn_norm_w.reshape(1, DN_HEAD_DIM), w_out.astype(jnp.bfloat16),
      gate.reshape(b, 1, d), ln_g.reshape(1, d), ln_b.reshape(1, d), shift2.reshape(b, 1, d),
      scale2.reshape(b, 1, d))


PS_TOKENS = 512


PEER_PAIRS = tuple((i, j) for i in range(PEER_TOPK) for j in range(PEER_TOPK) if (i + 1) * (j + 1) <= PEER_TOPK)
PEER_PAIR_ROWS = -(-len(PEER_PAIRS) // SUBLANES) * SUBLANES


def _top16_rows(s, n_rows):
    row = lax.broadcasted_iota(jnp.int32, (n_rows, LANES), 0)
    vals, idxs = [], []
    for _ in range(PEER_TOPK):
        m = jnp.max(s, axis=0, keepdims=True)
        i = jnp.min(jnp.where(s == m, row, n_rows), axis=0, keepdims=True)
        s = jnp.where(row == i, -jnp.inf, s)
        vals.append(m)
        idxs.append(i)
    return vals, idxs


def _peer_select_kernel(x_ref, wq_ref, sk_ref, exp_ref, gate_ref, s_scr):
    q = jnp.dot(x_ref[...].astype(jnp.bfloat16), wq_ref[...].astype(jnp.bfloat16),
                preferred_element_type=jnp.float32)
    for x in range(2):
        qx = q[:, x * PEER_KEY_HALF:(x + 1) * PEER_KEY_HALF].astype(jnp.bfloat16)
        s_scr[x] = lax.dot_general(sk_ref[x].astype(jnp.bfloat16), qx, (((1,), (1,)), ((), ())),
                                   preferred_element_type=jnp.float32)

    def chunk(c, carry):
        off = pl.multiple_of(c * LANES, LANES)
        s1, i1 = _top16_rows(s_scr[0, :, pl.ds(off, LANES)], PEER_N_KEYS)
        s2, i2 = _top16_rows(s_scr[1, :, pl.ds(off, LANES)], PEER_N_KEYS)
        pad = [jnp.full((1, LANES), -jnp.inf, jnp.float32)] * (PEER_PAIR_ROWS - len(PEER_PAIRS))
        cand = jnp.concatenate([s1[i] + s2[j] for i, j in PEER_PAIRS] + pad, axis=0)
        best, pos = _top16_rows(cand, PEER_PAIR_ROWS)
        best = jnp.concatenate(best, axis=0)
        pos = jnp.concatenate(pos, axis=0)
        k1 = jnp.zeros_like(pos)
        k2 = jnp.zeros_like(pos)
        for r, (i, j) in enumerate(PEER_PAIRS):
            hit = pos == r
            k1 = jnp.where(hit, i1[i], k1)
            k2 = jnp.where(hit, i2[j], k2)
        exp_ref[:, pl.ds(off, LANES)] = k1 * PEER_N_KEYS + k2
        e = jnp.exp(best - best[0:1, :])
        gate_ref[:, pl.ds(off, LANES)] = e / jnp.sum(e, axis=0, keepdims=True)
        return carry

    lax.fori_loop(0, PS_TOKENS // LANES, chunk, 0)


def peer_select(x, w_query, sub_keys):
    t, d = x.shape
    hw = 2 * PEER_KEY_HALF
    return pl.pallas_call(
        _peer_select_kernel,
        grid=(PEER_HEADS, t // PS_TOKENS),
        in_specs=[
            pl.BlockSpec((PS_TOKENS, d), lambda h, i: (i, 0)),
            pl.BlockSpec((d, hw), lambda h, i: (0, h)),
            pl.BlockSpec((2, PEER_N_KEYS, PEER_KEY_HALF), lambda h, i: (0, 0, 0)),
        ],
        out_specs=[
            pl.BlockSpec((PEER_TOPK, PS_TOKENS), lambda h, i: (h, i)),
            pl.BlockSpec((PEER_TOPK, PS_TOKENS), lambda h, i: (h, i)),
        ],
        out_shape=[
            jax.ShapeDtypeStruct((PEER_HEADS * PEER_TOPK, t), jnp.int32),
            jax.ShapeDtypeStruct((PEER_HEADS * PEER_TOPK, t), jnp.float32),
        ],
        scratch_shapes=[pltpu.VMEM((2, PEER_N_KEYS, PS_TOKENS), jnp.float32)],
        name="peer_select",
    )(x, w_query, sub_keys)


PEER_SEL = PEER_HEADS * PEER_TOPK
PG_TOKENS = 128
PG_SLOTS = 8
PG_AHEAD = PG_SLOTS - 1
N_DMA_QUEUES = 2
_SQRT_HALF = 0.7071067811865476


def _peer_eval_kernel(idx_ref, x_ref, gt_ref, uv_hbm, o_ref, buf, sem):
    n_chunks = x_ref.shape[-1] // LANES

    def issue(t, slot):
        for e in range(PEER_SEL):
            row = idx_ref[t * PEER_SEL + e]
            pltpu.make_async_copy(uv_hbm.at[row], buf.at[slot, :, e, :], sem.at[slot]).start(
                priority=e % N_DMA_QUEUES)

    def wait(slot):
        pltpu.make_async_copy(buf.at[slot], buf.at[slot], sem.at[slot]).wait()

    @pl.when(pl.program_id(0) == 0)
    def _():
        for s in range(PG_AHEAD):
            issue(s, s)

    lane = lax.broadcasted_iota(jnp.int32, (PEER_SEL, PG_TOKENS), 1)

    def group(g, carry):
        for s in range(PG_SLOTS):
            t = g * PG_SLOTS + s
            issue(t + PG_AHEAD, (s + PG_AHEAD) % PG_SLOTS)
            wait(s)
            x = x_ref[pl.ds(t, 1), :]
            p = buf[s, 0] * x[:, :LANES]
            for c in range(1, n_chunks):
                p = p + buf[s, c] * x[:, c * LANES:(c + 1) * LANES]
            a = jnp.sum(p, axis=-1, keepdims=True)
            act = 0.5 * a * (1.0 + lax.erf(a * _SQRT_HALF))
            gate = jnp.sum(jnp.where(lane == t, gt_ref[...], 0.0), axis=-1, keepdims=True)
            w = gate * act
            o_ref[pl.ds(t, 1), :] = jnp.concatenate(
                [jnp.sum(w * buf[s, n_chunks + c], axis=0, keepdims=True) for c in range(n_chunks)], axis=-1)
        return carry

    lax.fori_loop(0, PG_TOKENS // PG_SLOTS, group, 0)

    @pl.when(pl.program_id(0) == pl.num_programs(0) - 1)
    def _():
        for s in range(PG_AHEAD):
            wait(s)


def peer_eval(x, expert, gate_t, uv):
    t, d = x.shape
    blocks = expert.reshape(t // PG_TOKENS, PG_TOKENS, PEER_SEL)
    ahead = jnp.roll(blocks[:, :PG_SLOTS], -1, axis=0)
    expert = jnp.concatenate([blocks, ahead], axis=1).reshape(-1)
    return pl.pallas_call(
        _peer_eval_kernel,
        grid=(t // PG_TOKENS,),
        in_specs=[
            pl.BlockSpec(((PG_TOKENS + PG_SLOTS) * PEER_SEL,), lambda i: (i,), memory_space=pltpu.SMEM),
            pl.BlockSpec((PG_TOKENS, d), lambda i: (i, 0)),
            pl.BlockSpec((PEER_SEL, PG_TOKENS), lambda i: (0, i)),
            pl.BlockSpec(memory_space=pl.ANY),
        ],
        out_specs=pl.BlockSpec((PG_TOKENS, d), lambda i: (i, 0)),
        out_shape=jax.ShapeDtypeStruct((t, d), jnp.float32),
        scratch_shapes=[
            pltpu.VMEM((PG_SLOTS, 2 * d // LANES, PEER_SEL, LANES), jnp.float32),
            pltpu.SemaphoreType.DMA((PG_SLOTS,)),
        ],
        compiler_params=pltpu.CompilerParams(dimension_semantics=("arbitrary",)),
        name="peer_eval",
    )(expert, x, gate_t, uv)


def peer(h, w_query, sub_keys, u_tab, v_tab):
    b, n, d = h.shape
    x2 = h.reshape(b * n, d)
    expert_t, gate_t = peer_select(x2, w_query, sub_keys)
    uv = jnp.concatenate([u_tab, v_tab], axis=-1).reshape(PEER_EXPERTS, 2 * d // LANES, LANES)
    out = peer_eval(x2, expert_t.T, gate_t, uv)
    return out.reshape(b, n, d)


LN_ROWS = 512


def _res_ln_kernel(x_ref, y_ref, gate_ref, g_ref, b_ref, o_ref):
    t = DEEPNORM_ALPHA * x_ref[0] + gate_ref[0] * y_ref[0]
    o_ref[0] = _standardize_rows(t) * g_ref[...] + b_ref[...]


def res_ln(x, y, gate, g, b):
    bsz, n, d = x.shape
    return pl.pallas_call(
        _res_ln_kernel,
        grid=(bsz, n // LN_ROWS),
        in_specs=[
            pl.BlockSpec((1, LN_ROWS, d), lambda i, j: (i, j, 0)),
            pl.BlockSpec((1, LN_ROWS, d), lambda i, j: (i, j, 0)),
            pl.BlockSpec((1, 1, d), lambda i, j: (i, 0, 0)),
            pl.BlockSpec((1, d), lambda i, j: (0, 0)),
            pl.BlockSpec((1, d), lambda i, j: (0, 0)),
        ],
        out_specs=pl.BlockSpec((1, LN_ROWS, d), lambda i, j: (i, j, 0)),
        out_shape=jax.ShapeDtypeStruct(x.shape, x.dtype),
        name="res_ln",
    )(x, y, gate.reshape(bsz, 1, d), g.reshape(1, d), b.reshape(1, d))


def delta_net(qkv_d, gb, conv_w, a_log, dt_bias, s0):
    q, k, v, gates = gdn_prep(qkv_d, gb, conv_w, a_log, dt_bias)
    return gdn_scan(s0, *gdn_chunks(q, k, v, gates))


def kernel(x, c, ctx, c_ctx, w_ada, b_ada, w_in, conv_w, a_log, dt_bias, sink, dn_norm_w, w_out,
           ln1_g, ln1_b, peer_wq, peer_sub_keys, peer_u, peer_v, ln2_g, ln2_b):
    assert w_ada.shape[0] == DEPTH == 1
    l = 0
    b, n, d = x.shape
    m = ctx.shape[1]
    cond = jnp.concatenate([c, c_ctx[None], jnp.zeros((SUBLANES - b - 1, d), c.dtype)], axis=0)
    mod = ada_mod(cond, w_ada[l], b_ada[l])
    sh1, sc1, gt1, sh2, sc2, gt2 = jnp.split(mod[:b], 6, axis=-1)
    csh1, csc1 = (jnp.broadcast_to(t, (b, d)) for t in jnp.split(mod[b], 6, axis=-1)[:2])
    w_pad = jnp.pad(w_in[l], ((0, 0), (0, IN_PAD_W - D_IN))).astype(jnp.bfloat16)
    qkv, dn, z, gb = in_proj(x, sh1, sc1, w_pad, *rope_tables(n, True))
    qkv_c, dn_c, _, gb_c = in_proj(ctx, csh1, csc1, w_pad, *rope_tables(m, False))
    attn = window_attention(qkv, qkv_c, sink[l])
    s0 = jnp.zeros((b, N_DIRS, DN_HEADS, DN_HEAD_DIM, DN_HEAD_DIM), jnp.float32)
    _, _, s_ctx = delta_net(dn_c, gb_c, conv_w[l], a_log[l], dt_bias[l], s0)
    o_f, o_b, _ = delta_net(dn, gb, conv_w[l], a_log[l], dt_bias[l], s_ctx)
    x1, h2 = out_proj(x, attn, o_f, o_b, z, dn_norm_w[l], w_out[l], gt1, ln1_g[l], ln1_b[l], sh2, sc2)
    p = peer(h2, peer_wq[l], peer_sub_keys[l], peer_u[l], peer_v[l])
    return res_ln(x1, p, gt2, ln2_g[l], ln2_b[l])
```
